```python
import jax, jax.numpy as jnp
from jax import lax
import numpy as np

D_MODEL = 1024
BATCH = 4
SEQ = 8192
DEPTH = 1

CHUNK = 64
HEAD_DIM = 64
SB_HEADS = 8
FOX_HEADS = 8
SB_WIDTH = SB_HEADS * HEAD_DIM
FOX_WIDTH = FOX_HEADS * HEAD_DIM
MIX_WIDTH = SB_WIDTH + FOX_WIDTH
QUERY_BLOCK = 128
OFF_Q_SB = 0
OFF_K_SB = OFF_Q_SB + SB_WIDTH
OFF_V_SB = OFF_K_SB + SB_WIDTH
OFF_Q_FX = OFF_V_SB + SB_WIDTH
OFF_K_FX = OFF_Q_FX + FOX_WIDTH
OFF_V_FX = OFF_K_FX + FOX_WIDTH
OFF_F = OFF_V_FX + FOX_WIDTH
IN_COLS = OFF_F + FOX_HEADS
N_GROUPS = 4
EXPERTS_PER_GROUP = 8
N_EXPERTS = N_GROUPS * EXPERTS_PER_GROUP
TOP_K = 2
D_EXPERT = 512
DISPATCH_BLOCK = 128
PLE_DIM = 256
ALPHA = (2 * DEPTH) ** 0.25
BETA_INIT = (8 * DEPTH) ** -0.25
LN_EPS = 1e-5

kernel_name = 'sb_fox_hier_moe_deepnorm_ple_layer'


def layer_norm(x, g, b):
    xf = x.astype(jnp.float32)
    mu = jnp.mean(xf, axis=-1, keepdims=True)
    var = jnp.mean(jnp.square(xf - mu), axis=-1, keepdims=True)
    return ((xf - mu) * lax.rsqrt(var + LN_EPS) * g + b).astype(x.dtype)


def stick_breaking_block(q, k, v, t0):
    n_q, n_k = q.shape[2], k.shape[2]
    z = jnp.einsum('bhqd,bhkd->bhqk', q, k).astype(jnp.float32) * (HEAD_DIM ** -0.5)
    t_pos = t0 + jnp.arange(n_q)[:, None]
    s_pos = jnp.arange(n_k)[None, :]
    earlier = s_pos < t_pos
    log_keep = jnp.where(earlier, jax.nn.log_sigmoid(-z), 0.0)
    after_s = lax.cumsum(log_keep, axis=3, reverse=True) - log_keep
    w = jnp.where(earlier, jnp.exp(jax.nn.log_sigmoid(z) + after_s), 0.0)
    return jnp.einsum('bhqk,bhkd->bhqd', w.astype(v.dtype), v)


def forgetting_block(q, k, v, cf_q, cf_k, t0):
    n_q, n_k = q.shape[2], k.shape[2]
    z = jnp.einsum('bhqd,bhkd->bhqk', q, k).astype(jnp.float32) * (HEAD_DIM ** -0.5)
    z = z + cf_q[..., :, None] - cf_k[..., None, :]
    visible = jnp.arange(n_k)[None, :] <= (t0 + jnp.arange(n_q)[:, None])
    probs = jax.nn.softmax(jnp.where(visible, z, -jnp.inf), axis=-1)
    return jnp.einsum('bhqk,bhkd->bhqd', probs.astype(v.dtype), v)


def hybrid_token_mixer(x, w_in, b_forget, w_out):
    bsz, seq, _ = x.shape
    proj = jnp.einsum('bsd,de->bse', x, w_in)

    def heads(lo, n_heads):
        t = proj[..., lo:lo + n_heads * HEAD_DIM]
        return t.reshape(bsz, seq, n_heads, HEAD_DIM).transpose(0, 2, 1, 3)

    q_sb, k_sb, v_sb = heads(OFF_Q_SB, SB_HEADS), heads(OFF_K_SB, SB_HEADS), heads(OFF_V_SB, SB_HEADS)
    q_fx, k_fx, v_fx = heads(OFF_Q_FX, FOX_HEADS), heads(OFF_K_FX, FOX_HEADS), heads(OFF_V_FX, FOX_HEADS)
    f_logit = proj[..., OFF_F:OFF_F + FOX_HEADS].astype(jnp.float32) + b_forget.astype(jnp.float32)
    cum_log_f = jnp.cumsum(jax.nn.log_sigmoid(f_logit), axis=1).transpose(0, 2, 1)

    sb_out, fx_out = [], []
    for t0 in range(0, seq, QUERY_BLOCK):
        t1 = t0 + QUERY_BLOCK
        sb_out.append(stick_breaking_block(q_sb[:, :, t0:t1], k_sb[:, :, :t1], v_sb[:, :, :t1], t0))
        fx_out.append(forgetting_block(q_fx[:, :, t0:t1], k_fx[:, :, :t1], v_fx[:, :, :t1],
                                       cum_log_f[:, :, t0:t1], cum_log_f[:, :, :t1], t0))

    def merge(blocks):
        o = jnp.concatenate(blocks, axis=2)
        return o.transpose(0, 2, 1, 3).reshape(bsz, seq, -1)

    mixed = jnp.concatenate([merge(sb_out), merge(fx_out)], axis=-1)
    return jnp.einsum('bse,ed->bsd', mixed, w_out)


def hierarchical_moe(x_tok, w_group, b_group, w_router, b_router, w_gate, w_up, w_down):
    n, d = x_tok.shape
    xf = x_tok.astype(jnp.float32)
    g_logits = xf @ w_group.astype(jnp.float32) + b_group.astype(jnp.float32)
    g_prob = jax.nn.softmax(g_logits, axis=-1)
    g_sel = jnp.argmax(g_logits, axis=-1).astype(jnp.int32)
    e_logits = (xf @ w_router.astype(jnp.float32) + b_router.astype(jnp.float32))
    e_logits = e_logits.reshape(n, N_GROUPS, EXPERTS_PER_GROUP)
    e_in_group = jnp.take_along_axis(e_logits, g_sel[:, None, None], axis=1)[:, 0]
    top_val, top_loc = lax.top_k(e_in_group, TOP_K)
    gate = jax.nn.softmax(top_val, axis=-1) * jnp.take_along_axis(g_prob, g_sel[:, None], axis=1)
    expert_id = (g_sel[:, None] * EXPERTS_PER_GROUP + top_loc).astype(jnp.int32)

    m = n * TOP_K
    flat_e = expert_id.reshape(m)
    flat_tok = jnp.arange(m, dtype=jnp.int32) // TOP_K
    flat_w = gate.reshape(m)
    order = jnp.argsort(flat_e)
    e_sorted = flat_e[order]
    counts = jnp.bincount(flat_e, length=N_EXPERTS)
    starts = jnp.cumsum(counts) - counts
    padded = ((counts + DISPATCH_BLOCK - 1) // DISPATCH_BLOCK) * DISPATCH_BLOCK
    pad_end = jnp.cumsum(padded)
    pad_start = pad_end - padded
    dest = pad_start[e_sorted] + (jnp.arange(m, dtype=jnp.int32) - starts[e_sorted])
    cap = m + N_EXPERTS * DISPATCH_BLOCK
    n_blk = cap // DISPATCH_BLOCK
    tok_buf = jnp.zeros((cap,), jnp.int32).at[dest].set(flat_tok[order])
    w_buf = jnp.zeros((cap,), jnp.float32).at[dest].set(flat_w[order])
    blk_expert = jnp.minimum(
        jnp.searchsorted(pad_end, jnp.arange(n_blk) * DISPATCH_BLOCK, side='right'),
        N_EXPERTS - 1).astype(jnp.int32)

    def run_block(args):
        tok, e = args
        xb = x_tok[tok]
        hdn = jax.nn.silu(xb @ w_gate[e]) * (xb @ w_up[e])
        return hdn @ w_down[e]

    y = lax.map(run_block, (tok_buf.reshape(n_blk, DISPATCH_BLOCK), blk_expert))
    y = y.reshape(cap, d) * w_buf[:, None].astype(y.dtype)
    return jnp.zeros_like(x_tok).at[tok_buf].add(y)


def setup_inputs(seed: int = 0) -> dict:
    key = jax.random.key(seed)
    ks = jax.random.split(key, 20)
    nrm = jax.random.normal
    col_scale = np.ones((IN_COLS,), np.float32)
    col_scale[OFF_V_SB:OFF_V_SB + SB_WIDTH] = BETA_INIT
    col_scale[OFF_V_FX:OFF_V_FX + FOX_WIDTH] = BETA_INIT
    return {
        'x': nrm(ks[0], (BATCH, SEQ, D_MODEL), jnp.float32),
        'p': nrm(ks[1], (DEPTH, BATCH, SEQ, PLE_DIM), jnp.float32),
        'w_in': nrm(ks[2], (DEPTH, D_MODEL, IN_COLS), jnp.float32) * (D_MODEL ** -0.5) * jnp.asarray(col_scale),
        'b_forget': 3.0 + 0.1 * nrm(ks[3], (DEPTH, FOX_HEADS), jnp.float32),
        'w_out': nrm(ks[4], (DEPTH, MIX_WIDTH, D_MODEL), jnp.float32) * (MIX_WIDTH ** -0.5) * BETA_INIT,
        'ln_mix_g': 1.0 + 0.02 * nrm(ks[5], (DEPTH, D_MODEL), jnp.float32),
        'ln_mix_b': 0.02 * nrm(ks[6], (DEPTH, D_MODEL), jnp.float32),
        'w_group': nrm(ks[7], (DEPTH, D_MODEL, N_GROUPS), jnp.float32) * (D_MODEL ** -0.5),
        'b_group': 0.01 * nrm(ks[8], (DEPTH, N_GROUPS), jnp.float32),
        'w_router': nrm(ks[9], (DEPTH, D_MODEL, N_EXPERTS), jnp.float32) * (D_MODEL ** -0.5),
        'b_router': 0.01 * nrm(ks[10], (DEPTH, N_EXPERTS), jnp.float32),
        'w_gate': nrm(ks[11], (DEPTH, N_EXPERTS, D_MODEL, D_EXPERT), jnp.float32) * (D_MODEL ** -0.5),
        'w_up': nrm(ks[12], (DEPTH, N_EXPERTS, D_MODEL, D_EXPERT), jnp.float32) * (D_MODEL ** -0.5) * BETA_INIT,
        'w_down': nrm(ks[13], (DEPTH, N_EXPERTS, D_EXPERT, D_MODEL), jnp.float32) * (D_EXPERT ** -0.5) * BETA_INIT,
        'w_ple': nrm(ks[14], (DEPTH, PLE_DIM, D_MODEL), jnp.float32) * (PLE_DIM ** -0.5),
        'w_ple_gate': nrm(ks[15], (DEPTH, D_MODEL, D_MODEL), jnp.float32) * (D_MODEL ** -0.5),
        'ln_ffn_g': 1.0 + 0.02 * nrm(ks[16], (DEPTH, D_MODEL), jnp.float32),
        'ln_ffn_b': 0.02 * nrm(ks[17], (DEPTH, D_MODEL), jnp.float32),
    }


def reference(x, p, w_in, b_forget, w_out, ln_mix_g, ln_mix_b, w_group, b_group, w_router, b_router,
              w_gate, w_up, w_down, w_ple, w_ple_gate, ln_ffn_g, ln_ffn_b):
    bsz, seq, d = x.shape
    h = x
    for i in range(DEPTH):
        mix = hybrid_token_mixer(h, w_in[i], b_forget[i], w_out[i])
        h = layer_norm(ALPHA * h + mix, ln_mix_g[i], ln_mix_b[i])
        ffn = hierarchical_moe(h.reshape(bsz * seq, d), w_group[i], b_group[i], w_router[i], b_router[i],
                               w_gate[i], w_up[i], w_down[i]).reshape(bsz, seq, d)
        ple = jnp.einsum('bse,ed->bsd', p[i], w_ple[i]) * jax.nn.sigmoid(jnp.einsum('bsd,de->bse', h, w_ple_gate[i]))
        h = layer_norm(ALPHA * h + ffn + ple, ln_ffn_g[i], ln_ffn_b[i])
    return h
```

```python
import functools

import jax
import jax.numpy as jnp
from jax import lax
from jax.experimental import pallas as pl
from jax.experimental.pallas import tpu as pltpu

F32 = jnp.float32
BF16 = jnp.bfloat16

HEAD_DIM = 64
SB_HEADS = 8
FOX_HEADS = 8
SB_WIDTH = SB_HEADS * HEAD_DIM
FOX_WIDTH = FOX_HEADS * HEAD_DIM
QKV_COLS = 3 * (SB_WIDTH + FOX_WIDTH)
N_GROUPS = 4
EXPERTS_PER_GROUP = 8
N_EXPERTS = N_GROUPS * EXPERTS_PER_GROUP
TOP_K = 2
LN_EPS = 1e-5
SCALE = HEAD_DIM ** -0.5

LANES = 128
ATT_BLOCK = 256
ROW_TILE = 512
POST_TILE = 256
MOE_BLOCK = 256
MOVE_TILE = 256
ROUTE_OFF = N_GROUPS
VMEM_LIMIT = 56 * 1024 * 1024


def _cparams(*sem):
    return pltpu.CompilerParams(dimension_semantics=sem, vmem_limit_bytes=VMEM_LIMIT)


def _dot(a, b):
    return jnp.dot(a, b, preferred_element_type=F32)


def _dot_nt(a, b):
    return lax.dot_general(a, b, (((1,), (1,)), ((), ())), preferred_element_type=F32)


def _split2(x):
    hi = x.astype(BF16)
    lo = (x - hi.astype(F32)).astype(BF16)
    return hi, lo


def _split3(x):
    hi = x.astype(BF16)
    r = x - hi.astype(F32)
    mid = r.astype(BF16)
    lo = (r - mid.astype(F32)).astype(BF16)
    return hi, mid, lo


def _layer_norm(u, g, b):
    mu = jnp.mean(u, axis=-1, keepdims=True)
    d = u - mu
    var = jnp.mean(d * d, axis=-1, keepdims=True)
    return d * lax.rsqrt(var + LN_EPS) * g + b


def _proj_kernel(x_ref, w_ref, bf_ref, tri_ref, qkv_ref, cf_ref, carry_ref, *, tiles_per_seq):
    i = pl.program_id(0)

    @pl.when(i % tiles_per_seq == 0)
    def _():
        carry_ref[...] = jnp.zeros_like(carry_ref)

    xb = x_ref[...].astype(BF16)
    chunk = 512
    for c in range(QKV_COLS // chunk):
        sl = slice(c * chunk, (c + 1) * chunk)
        qkv_ref[:, sl] = _dot(xb, w_ref[:, sl]).astype(BF16)

    fl = _dot(xb, w_ref[:, QKV_COLS:]) + bf_ref[...]
    lf = jnp.minimum(fl, 0.0) - jnp.log(1.0 + jnp.exp(-jnp.abs(fl)))
    tri = tri_ref[...]
    hi, mid, lo = _split3(lf)
    cum = _dot(tri, hi) + _dot(tri, mid) + _dot(tri, lo) + carry_ref[...]
    cf_ref[...] = cum
    carry_ref[...] = cum[-1:, :]


def _proj_call(x2, w_pad, bf_pad, seq):
    n, d = x2.shape
    tm = min(ROW_TILE, seq)
    cols = w_pad.shape[1]
    row = lax.broadcasted_iota(jnp.int32, (tm, tm), 0)
    col = lax.broadcasted_iota(jnp.int32, (tm, tm), 1)
    tri = (col <= row).astype(BF16)
    return pl.pallas_call(
        functools.partial(_proj_kernel, tiles_per_seq=seq // tm),
        grid=(n // tm,),
        in_specs=[
            pl.BlockSpec((tm, d), lambda i: (i, 0)),
            pl.BlockSpec((d, cols), lambda i: (0, 0)),
            pl.BlockSpec((1, LANES), lambda i: (0, 0)),
            pl.BlockSpec((tm, tm), lambda i: (0, 0)),
        ],
        out_specs=[
            pl.BlockSpec((tm, QKV_COLS), lambda i: (i, 0)),
            pl.BlockSpec((tm, LANES), lambda i: (i, 0)),
        ],
        out_shape=[
            jax.ShapeDtypeStruct((n, QKV_COLS), BF16),
            jax.ShapeDtypeStruct((n, LANES), F32),
        ],
        scratch_shapes=[pltpu.VMEM((1, LANES), F32)],
        compiler_params=_cparams("arbitrary"),
        name="proj",
    )(x2, w_pad, bf_pad, tri)


def _softplus(z):
    return jnp.maximum(z, 0.0) + jnp.log(1.0 + jnp.exp(-jnp.abs(z)))


def _sb_kernel(q_ref, k_ref, v_ref, tri_ref, o_ref):
    t = ATT_BLOCK
    qi = pl.program_id(2)
    q = q_ref[0] * jnp.asarray(SCALE, BF16)
    lane = lax.broadcasted_iota(jnp.int32, (t, LANES), 1)
    row = lax.broadcasted_iota(jnp.int32, (t, t), 0)
    col = lax.broadcasted_iota(jnp.int32, (t, t), 1)
    earlier = col < row
    tri = tri_ref[...]
    zero_q = jnp.zeros_like(q)
    qs = (jnp.where(lane < HEAD_DIM, q, zero_q), jnp.where(lane >= HEAD_DIM, q, zero_q))

    def block(kb, state, diag):
        start = pl.multiple_of(kb * t, t)
        kblk = k_ref[0, pl.ds(start, t), :]
        vblk = v_ref[0, pl.ds(start, t), :]
        new = []
        for h in range(2):
            carry, acc = state[2 * h], state[2 * h + 1]
            z = _dot_nt(qs[h], kblk)
            sp = _softplus(z)
            if diag:
                sp = jnp.where(earlier, sp, 0.0)
            hi, lo = _split2(sp)
            cum = _dot(hi, tri) + _dot(lo, tri)
            w = jnp.exp(z - cum - carry)
            if diag:
                w = jnp.where(earlier, w, 0.0)
            acc = acc + _dot(w.astype(BF16), vblk)
            carry = carry + cum[:, 0:1]
            new += [carry, acc]
        return tuple(new)

    init = (jnp.zeros((t, 1), F32), jnp.zeros((t, LANES), F32)) * 2
    state = block(qi, init, True)
    state = lax.fori_loop(0, qi, lambda i, s: block(qi - 1 - i, s, False), state)
    o_ref[0] = jnp.where(lane < HEAD_DIM, state[1], state[3]).astype(o_ref.dtype)


def _att_tri():
    t = ATT_BLOCK
    s = lax.broadcasted_iota(jnp.int32, (t, t), 0)
    j = lax.broadcasted_iota(jnp.int32, (t, t), 1)
    return (s >= j).astype(BF16)


def _sb_call(qkv):
    b, seq, _ = qkv.shape
    t = ATT_BLOCK
    pairs = SB_WIDTH // LANES
    q_off, k_off, v_off = 0, SB_WIDTH // LANES, 2 * SB_WIDTH // LANES
    return pl.pallas_call(
        _sb_kernel,
        grid=(b, pairs, seq // t),
        in_specs=[
            pl.BlockSpec((1, t, LANES), lambda bi, hp, qi: (bi, qi, q_off + hp)),
            pl.BlockSpec((1, seq, LANES), lambda bi, hp, qi: (bi, 0, k_off + hp)),
            pl.BlockSpec((1, seq, LANES), lambda bi, hp, qi: (bi, 0, v_off + hp)),
            pl.BlockSpec((t, t), lambda bi, hp, qi: (0, 0)),
        ],
        out_specs=pl.BlockSpec((1, t, LANES), lambda bi, hp, qi: (bi, qi, hp)),
        out_shape=jax.ShapeDtypeStruct((b, seq, SB_WIDTH), BF16),
        compiler_params=_cparams("arbitrary", "arbitrary", "arbitrary"),
        name="sb_attn",
    )(qkv, qkv, qkv, _att_tri())


def _fox_kernel(q_ref, k_ref, v_ref, ck_ref, o_ref):
    t = ATT_BLOCK
    qi = pl.program_id(2)
    q = q_ref[0] * jnp.asarray(SCALE, BF16)
    lane = lax.broadcasted_iota(jnp.int32, (t, LANES), 1)
    row = lax.broadcasted_iota(jnp.int32, (t, t), 0)
    col = lax.broadcasted_iota(jnp.int32, (t, t), 1)
    visible = col <= row
    own = (lane < HEAD_DIM, lane >= HEAD_DIM)
    zero_q = jnp.zeros_like(q)
    qs = tuple(jnp.where(o, q, zero_q) for o in own)

    def block(kb, state, diag):
        start = pl.multiple_of(kb * t, t)
        kblk = k_ref[0, pl.ds(start, t), :]
        vblk = v_ref[0, pl.ds(start, t), :]
        one_v = jnp.ones_like(vblk)
        new = []
        for h in range(2):
            m, acc = state[2 * h], state[2 * h + 1]
            ck = ck_ref[0, h, pl.ds(kb, 1), :]
            s = _dot_nt(qs[h], kblk) - ck
            if diag:
                s = jnp.where(visible, s, -jnp.inf)
            m_new = jnp.maximum(m, jnp.max(s, axis=-1, keepdims=True))
            p = jnp.exp(s - m_new)
            vmod = jnp.where(own[h], vblk, one_v)
            acc = jnp.exp(m - m_new) * acc + _dot(p.astype(BF16), vmod)
            new += [m_new, acc]
        return tuple(new)

    init = (jnp.full((t, 1), -jnp.inf, F32), jnp.zeros((t, LANES), F32)) * 2
    state = block(qi, init, True)
    state = lax.fori_loop(0, qi, lambda i, s: block(qi - 1 - i, s, False), state)
    outs = [a / pltpu.roll(a, HEAD_DIM, 1) for a in (state[1], state[3])]
    o_ref[0] = jnp.where(own[0], outs[0], outs[1]).astype(o_ref.dtype)


def _fox_call(qkv, cf4):
    b, seq, _ = qkv.shape
    t = ATT_BLOCK
    pairs = FOX_WIDTH // LANES
    base = 3 * SB_WIDTH // LANES
    q_off, k_off, v_off = base, base + pairs, base + 2 * pairs
    return pl.pallas_call(
        _fox_kernel,
        grid=(b, pairs, seq // t),
        in_specs=[
            pl.BlockSpec((1, t, LANES), lambda bi, hp, qi: (bi, qi, q_off + hp)),
            pl.BlockSpec((1, seq, LANES), lambda bi, hp, qi: (bi, 0, k_off + hp)),
            pl.BlockSpec((1, seq, LANES), lambda bi, hp, qi: (bi, 0, v_off + hp)),
            pl.BlockSpec((1, 2, seq // t, t), lambda bi, hp, qi: (bi, hp, 0, 0)),
        ],
        out_specs=pl.BlockSpec((1, t, LANES), lambda bi, hp, qi: (bi, qi, hp)),
        out_shape=jax.ShapeDtypeStruct((b, seq, FOX_WIDTH), BF16),
        compiler_params=_cparams("arbitrary", "arbitrary", "arbitrary"),
        name="fox_attn",
    )(qkv, qkv, qkv, cf4)


def _post_kernel(sb_ref, fx_ref, x_ref, p_ref, wo_ref, g1_ref, b1_ref, wr_ref, br_ref, wpg_ref,
                 wp_ref, tri_ref, base_ref, h_ref, route_ref, cnt_ref, carry_ref, *, alpha):
    i = pl.program_id(0)

    @pl.when(i == 0)
    def _():
        carry_ref[...] = jnp.zeros_like(carry_ref)

    mix = _dot(sb_ref[...], wo_ref[:SB_WIDTH, :]) + _dot(fx_ref[...], wo_ref[SB_WIDTH:, :])
    h = _layer_norm(alpha * x_ref[...] + mix, g1_ref[...], b1_ref[...])
    h_ref[...] = h

    tm = h.shape[0]
    lane = lax.broadcasted_iota(jnp.int32, (tm, LANES), 1)
    big = jnp.int32(4 * LANES)
    logits = jnp.dot(h, wr_ref[...], preferred_element_type=F32,
                     precision=lax.Precision.HIGHEST) + br_ref[...]

    def first_argmax(vals):
        vmax = jnp.max(vals, axis=-1, keepdims=True)
        idx = jnp.min(jnp.where(vals == vmax, lane, big), axis=-1, keepdims=True)
        return vmax, idx

    neg = -jnp.inf
    gl = jnp.where(lane < N_GROUPS, logits, neg)
    gmax, gsel = first_argmax(gl)
    g_prob = 1.0 / jnp.sum(jnp.exp(gl - gmax), axis=-1, keepdims=True)
    lo = ROUTE_OFF + gsel * EXPERTS_PER_GROUP
    el = jnp.where((lane >= lo) & (lane < lo + EXPERTS_PER_GROUP), logits, neg)
    v1, i1 = first_argmax(el)
    v2, i2 = first_argmax(jnp.where(lane == i1, neg, el))
    e2 = jnp.exp(v2 - v1)
    gate1 = g_prob / (1.0 + e2)
    gate2 = g_prob * e2 / (1.0 + e2)

    hit1 = lane == i1
    hit2 = lane == i2
    onehot = jnp.where(hit1 | hit2, 1.0, 0.0)
    before = _dot(tri_ref[...], onehot.astype(BF16)) + carry_ref[...]
    rank1 = jnp.sum(jnp.where(hit1, before, 0.0), axis=-1, keepdims=True)
    rank2 = jnp.sum(jnp.where(hit2, before, 0.0), axis=-1, keepdims=True)
    total = before[-1:, :] + onehot[-1:, :]
    carry_ref[...] = total
    cnt_ref[...] = jnp.broadcast_to(total, cnt_ref.shape)

    route = jnp.zeros((tm, LANES), F32)
    fields = ((i1 - ROUTE_OFF).astype(F32), (i2 - ROUTE_OFF).astype(F32), gate1, gate2, rank1, rank2)
    for k, val in enumerate(fields):
        route = jnp.where(lane == k, val, route)
    route_ref[...] = route

    ple = _dot(p_ref[...].astype(BF16), wp_ref[...])
    gate = jax.nn.sigmoid(_dot(h.astype(BF16), wpg_ref[...]))
    base_ref[...] = alpha * h + ple * gate


def _post_call(sb, fx, x2, p2, wo, g1, b1, wr, br, wpg, wp, alpha):
    n, d = x2.shape
    tm = min(POST_TILE, n)
    row = lax.broadcasted_iota(jnp.int32, (tm, tm), 0)
    col = lax.broadcasted_iota(jnp.int32, (tm, tm), 1)
    tri = (col < row).astype(BF16)
    const = lambda i: (0, 0)
    tile = lambda i: (i, 0)
    return pl.pallas_call(
        functools.partial(_post_kernel, alpha=alpha),
        grid=(n // tm,),
        in_specs=[
            pl.BlockSpec((tm, SB_WIDTH), tile),
            pl.BlockSpec((tm, FOX_WIDTH), tile),
            pl.BlockSpec((tm, d), tile),
            pl.BlockSpec((tm, p2.shape[1]), tile),
            pl.BlockSpec(wo.shape, const),
            pl.BlockSpec((1, d), const),
            pl.BlockSpec((1, d), const),
            pl.BlockSpec(wr.shape, const),
            pl.BlockSpec((1, LANES), const),
            pl.BlockSpec(wpg.shape, const),
            pl.BlockSpec(wp.shape, const),
            pl.BlockSpec((tm, tm), const),
        ],
        out_specs=[
            pl.BlockSpec((tm, d), tile),
            pl.BlockSpec((tm, d), tile),
            pl.BlockSpec((tm, LANES), tile),
            pl.BlockSpec((8, LANES), const),
        ],
        out_shape=[
            jax.ShapeDtypeStruct((n, d), F32),
            jax.ShapeDtypeStruct((n, d), F32),
            jax.ShapeDtypeStruct((n, LANES), F32),
            jax.ShapeDtypeStruct((8, LANES), F32),
        ],
        scratch_shapes=[pltpu.VMEM((1, LANES), F32)],
        compiler_params=_cparams("arbitrary"),
        name="post_attn",
    )(sb, fx, x2, p2, wo, g1, b1, wr, br, wpg, wp, tri)


def _dispatch_kernel(dest_ref, h_ref, xs_in_ref, xs_ref, sem):
    del xs_in_ref
    tt = h_ref.shape[0]

    def copy(r, k):
        return pltpu.make_async_copy(h_ref.at[pl.ds(r, 1), :],
                                     xs_ref.at[pl.ds(dest_ref[TOP_K * r + k], 1), :], sem)

    def start(r, c):
        for k in range(TOP_K):
            copy(r, k).start()
        return c

    def wait(r, c):
        for k in range(TOP_K):
            copy(r, k).wait()
        return c

    lax.fori_loop(0, tt, start, 0)
    lax.fori_loop(0, tt, wait, 0)


def _dispatch_call(h, dest_flat, xs_zero):
    n, d = h.shape
    tt = min(MOVE_TILE, n)
    return pl.pallas_call(
        _dispatch_kernel,
        grid=(n // tt,),
        in_specs=[
            pl.BlockSpec((TOP_K * tt,), lambda i: (i,), memory_space=pltpu.SMEM),
            pl.BlockSpec((tt, d), lambda i: (i, 0)),
            pl.BlockSpec(memory_space=pl.ANY),
        ],
        out_specs=pl.BlockSpec(memory_space=pl.ANY),
        out_shape=jax.ShapeDtypeStruct(xs_zero.shape, xs_zero.dtype),
        scratch_shapes=[pltpu.SemaphoreType.DMA(())],
        input_output_aliases={2: 0},
        compiler_params=_cparams("arbitrary"),
        name="moe_dispatch",
    )(dest_flat, h, xs_zero)


def _expert_kernel(be_ref, nb_ref, xs_ref, wg_ref, wu_ref, wd_ref, ys_ref):
    i = pl.program_id(0)

    @pl.when(i < nb_ref[0])
    def _():
        xb = xs_ref[...].astype(BF16)
        g = _dot(xb, wg_ref[0])
        u = _dot(xb, wu_ref[0])
        hdn = (g * jax.nn.sigmoid(g)) * u
        ys_ref[...] = _dot(hdn.astype(BF16), wd_ref[0])

    @pl.when(i >= nb_ref[0])
    def _():
        ys_ref[...] = jnp.zeros_like(ys_ref)


def _expert_call(blk_expert, n_used, xs, wg, wu, wd):
    cap, d = xs.shape
    f = wg.shape[2]
    nblk = cap // MOE_BLOCK
    return pl.pallas_call(
        _expert_kernel,
        grid_spec=pltpu.PrefetchScalarGridSpec(
            num_scalar_prefetch=2,
            grid=(nblk,),
            in_specs=[
                pl.BlockSpec((MOE_BLOCK, d), lambda i, be, nb: (i, 0)),
                pl.BlockSpec((1, d, f), lambda i, be, nb: (be[i], 0, 0)),
                pl.BlockSpec((1, d, f), lambda i, be, nb: (be[i], 0, 0)),
                pl.BlockSpec((1, f, d), lambda i, be, nb: (be[i], 0, 0)),
            ],
            out_specs=pl.BlockSpec((MOE_BLOCK, d), lambda i, be, nb: (i, 0)),
        ),
        out_shape=jax.ShapeDtypeStruct((cap, d), F32),
        compiler_params=_cparams("arbitrary"),
        name="moe_experts",
    )(blk_expert, n_used, xs, wg, wu, wd)


def _combine_kernel(dest_ref, base_ref, route_ref, g2_ref, b2_ref, ys_ref, o_ref, buf_ref, sem):
    tt = base_ref.shape[0]

    def copy(r, k):
        return pltpu.make_async_copy(ys_ref.at[pl.ds(dest_ref[TOP_K * r + k], 1), :],
                                     buf_ref.at[k, pl.ds(r, 1), :], sem)

    def start(r, c):
        for k in range(TOP_K):
            copy(r, k).start()
        return c

    def wait(r, c):
        for k in range(TOP_K):
            copy(r, k).wait()
        return c

    lax.fori_loop(0, tt, start, 0)
    lax.fori_loop(0, tt, wait, 0)

    route = route_ref[...]
    ffn = route[:, 2:3] * buf_ref[0] + route[:, 3:4] * buf_ref[1]
    o_ref[...] = _layer_norm(base_ref[...] + ffn, g2_ref[...], b2_ref[...])


def _combine_call(dest_flat, base, route, g2, b2, ys):
    n, d = base.shape
    tt = min(MOVE_TILE, n)
    return pl.pallas_call(
        _combine_kernel,
        grid=(n // tt,),
        in_specs=[
            pl.BlockSpec((TOP_K * tt,), lambda i: (i,), memory_space=pltpu.SMEM),
            pl.BlockSpec((tt, d), lambda i: (i, 0)),
            pl.BlockSpec((tt, LANES), lambda i: (i, 0)),
            pl.BlockSpec((1, d), lambda i: (0, 0)),
            pl.BlockSpec((1, d), lambda i: (0, 0)),
            pl.BlockSpec(memory_space=pl.ANY),
        ],
        out_specs=pl.BlockSpec((tt, d), lambda i: (i, 0)),
        out_shape=jax.ShapeDtypeStruct((n, d), F32),
        scratch_shapes=[pltpu.VMEM((TOP_K, tt, d), F32), pltpu.SemaphoreType.DMA(())],
        compiler_params=_cparams("arbitrary"),
        name="moe_combine",
    )(dest_flat, base, route, g2, b2, ys)


def _layer(h, p_i, w_in, b_forget, w_out, g1, b1, w_group, b_group, w_router, b_router,
           w_gate, w_up, w_down, w_ple, w_ple_gate, g2, b2, alpha):
    bsz, seq, d = h.shape
    n = bsz * seq
    x2 = h.reshape(n, d)

    in_cols = w_in.shape[1]
    w_pad = jnp.pad(w_in, ((0, 0), (0, QKV_COLS + LANES - in_cols))).astype(BF16)
    bf_pad = jnp.pad(b_forget.astype(F32), (0, LANES - FOX_HEADS)).reshape(1, LANES)
    wr = jnp.pad(jnp.concatenate([w_group, w_router], axis=1).astype(F32),
                 ((0, 0), (0, LANES - N_GROUPS - N_EXPERTS)))
    br = jnp.pad(jnp.concatenate([b_group, b_router]).astype(F32),
                 (0, LANES - N_GROUPS - N_EXPERTS)).reshape(1, LANES)

    qkv, cf = _proj_call(x2, w_pad, bf_pad, seq)
    qkv3 = qkv.reshape(bsz, seq, QKV_COLS)
    cf4 = cf[:, :FOX_HEADS].reshape(bsz, seq, FOX_HEADS).transpose(0, 2, 1)
    cf4 = cf4.reshape(bsz, FOX_HEADS, seq // ATT_BLOCK, ATT_BLOCK)

    sb = _sb_call(qkv3).reshape(n, SB_WIDTH)
    fx = _fox_call(qkv3, cf4).reshape(n, FOX_WIDTH)

    base, h1, route, counts = _post_call(
        sb, fx, x2, p_i.reshape(n, -1), w_out.astype(BF16), g1.reshape(1, d), b1.reshape(1, d),
        wr, br, w_ple_gate.astype(BF16), w_ple.astype(BF16), alpha)

    counts = counts[0, ROUTE_OFF:ROUTE_OFF + N_EXPERTS].astype(jnp.int32)
    padded = ((counts + MOE_BLOCK - 1) // MOE_BLOCK) * MOE_BLOCK
    pad_end = jnp.cumsum(padded)
    pad_start = pad_end - padded
    eid = route[:, 0:TOP_K].astype(jnp.int32)
    rank = route[:, 4:4 + TOP_K].astype(jnp.int32)
    dest = (pad_start[eid] + rank).reshape(n * TOP_K)
    cap = n * TOP_K + N_EXPERTS * MOE_BLOCK
    nblk = cap // MOE_BLOCK
    blk_expert = jnp.minimum(
        jnp.searchsorted(pad_end, jnp.arange(nblk, dtype=jnp.int32) * MOE_BLOCK, side='right'),
        N_EXPERTS - 1).astype(jnp.int32)
    n_used = (pad_end[-1:] // MOE_BLOCK).astype(jnp.int32)

    xs = _dispatch_call(h1, dest, jnp.zeros((cap, d), F32))
    ys = _expert_call(blk_expert, n_used, xs, w_gate.astype(BF16), w_up.astype(BF16),
                      w_down.astype(BF16))
    out = _combine_call(dest, base, route, g2.reshape(1, d), b2.reshape(1, d), ys)
    return out.reshape(bsz, seq, d)


def kernel(x, p, w_in, b_forget, w_out, ln_mix_g, ln_mix_b, w_group, b_group, w_router, b_router,
           w_gate, w_up, w_down, w_ple, w_ple_gate, ln_ffn_g, ln_ffn_b):
    depth = w_in.shape[0]
    alpha = (2 * depth) ** 0.25
    h = x
    for i in range(depth):
        h = _layer(h, p[i], w_in[i], b_forget[i], w_out[i], ln_mix_g[i], ln_mix_b[i], w_group[i],
                   b_group[i], w_router[i], b_router[i], w_gate[i], w_up[i], w_down[i], w_ple[i],
                   w_ple_gate[i], ln_ffn_g[i], ln_ffn_b[i], alpha)
    return h
```

```python
import functools

import jax
import jax.numpy as jnp
from jax import lax
from jax.experimental import pallas as pl
from jax.experimental.pallas import tpu as pltpu

F32 = jnp.float32
BF16 = jnp.bfloat16

HEAD_DIM = 64
SB_HEADS = 8
FOX_HEADS = 8
SB_WIDTH = SB_HEADS * HEAD_DIM
FOX_WIDTH = FOX_HEADS * HEAD_DIM
QKV_COLS = 3 * (SB_WIDTH + FOX_WIDTH)
N_GROUPS = 4
EXPERTS_PER_GROUP = 8
N_EXPERTS = N_GROUPS * EXPERTS_PER_GROUP
TOP_K = 2
LN_EPS = 1e-5
SCALE = HEAD_DIM ** -0.5

LANES = 128
ATT_BLOCK = 256
ROW_TILE = 512
POST_TILE = 256
MOE_BLOCK = 256
MOVE_TILE = 256
ROUTE_OFF = N_GROUPS
VMEM_LIMIT = 56 * 1024 * 1024


def _cparams(*sem):
    return pltpu.CompilerParams(dimension_semantics=sem, vmem_limit_bytes=VMEM_LIMIT)


def _dot(a, b):
    return jnp.dot(a, b, preferred_element_type=F32)


def _dot_nt(a, b):
    return lax.dot_general(a, b, (((1,), (1,)), ((), ())), preferred_element_type=F32)


def _split2(x):
    hi = x.astype(BF16)
    lo = (x - hi.astype(F32)).astype(BF16)
    return hi, lo


def _split3(x):
    hi = x.astype(BF16)
    r = x - hi.astype(F32)
    mid = r.astype(BF16)
    lo = (r - mid.astype(F32)).astype(BF16)
    return hi, mid, lo


def _layer_norm(u, g, b):
    mu = jnp.mean(u, axis=-1, keepdims=True)
    d = u - mu
    var = jnp.mean(d * d, axis=-1, keepdims=True)
    return d * lax.rsqrt(var + LN_EPS) * g + b


def _proj_kernel(x_ref, w_ref, bf_ref, tri_ref, qkv_ref, cf_ref, carry_ref, *, tiles_per_seq):
    i = pl.program_id(0)

    @pl.when(i % tiles_per_seq == 0)
    def _():
        carry_ref[...] = jnp.zeros_like(carry_ref)

    xb = x_ref[...].astype(BF16)
    chunk = 512
    for c in range(QKV_COLS // chunk):
        sl = slice(c * chunk, (c + 1) * chunk)
        qkv_ref[:, sl] = _dot(xb, w_ref[:, sl]).astype(BF16)

    fl = _dot(xb, w_ref[:, QKV_COLS:]) + bf_ref[...]
    lf = jnp.minimum(fl, 0.0) - jnp.log(1.0 + jnp.exp(-jnp.abs(fl)))
    tri = tri_ref[...]
    hi, mid, lo = _split3(lf)
    cum = _dot(tri, hi) + _dot(tri, mid) + _dot(tri, lo) + carry_ref[...]
    cf_ref[...] = cum
    carry_ref[...] = cum[-1:, :]


def _proj_call(x2, w_pad, bf_pad, seq):
    n, d = x2.shape
    tm = min(ROW_TILE, seq)
    cols = w_pad.shape[1]
    row = lax.broadcasted_iota(jnp.int32, (tm, tm), 0)
    col = lax.broadcasted_iota(jnp.int32, (tm, tm), 1)
    tri = (col <= row).astype(BF16)
    return pl.pallas_call(
        functools.partial(_proj_kernel, tiles_per_seq=seq // tm),
        grid=(n // tm,),
        in_specs=[
            pl.BlockSpec((tm, d), lambda i: (i, 0)),
            pl.BlockSpec((d, cols), lambda i: (0, 0)),
            pl.BlockSpec((1, LANES), lambda i: (0, 0)),
            pl.BlockSpec((tm, tm), lambda i: (0, 0)),
        ],
        out_specs=[
            pl.BlockSpec((tm, QKV_COLS), lambda i: (i, 0)),
            pl.BlockSpec((tm, LANES), lambda i: (i, 0)),
        ],
        out_shape=[
            jax.ShapeDtypeStruct((n, QKV_COLS), BF16),
            jax.ShapeDtypeStruct((n, LANES), F32),
        ],
        scratch_shapes=[pltpu.VMEM((1, LANES), F32)],
        compiler_params=_cparams("arbitrary"),
        name="proj",
    )(x2, w_pad, bf_pad, tri)


ZERO_EXP = 110.0


def _softplus(z):
    return jnp.maximum(z, 0.0) + jnp.log(1.0 + jnp.exp(-jnp.abs(z)))


def _key_absmax(k_ref, kam_ref):
    seq = k_ref.shape[1]
    rows = min(512, seq)

    def body(i, m):
        blk = k_ref[0, pl.ds(pl.multiple_of(i * rows, rows), rows), :]
        return jnp.maximum(m, jnp.max(jnp.abs(blk.astype(F32)), axis=0, keepdims=True))

    kam = lax.fori_loop(0, seq // rows, body, jnp.zeros((1, LANES), F32))
    kam_ref[...] = jnp.broadcast_to(kam, kam_ref.shape)


def _score_bound(qh, kam):
    zmax = jnp.sum(jnp.abs(qh.astype(F32)) * kam, axis=-1, keepdims=True)
    return zmax * (1.0 + 2.0 ** -10) + 2.0 ** -10


def _sb_kernel(q_ref, k_ref, v_ref, tri_ref, o_ref, kam_ref):
    t = ATT_BLOCK
    qi = pl.program_id(2)

    @pl.when(qi == 0)
    def _():
        _key_absmax(k_ref, kam_ref)

    q = q_ref[0] * jnp.asarray(SCALE, BF16)
    lane = lax.broadcasted_iota(jnp.int32, (t, LANES), 1)
    row = lax.broadcasted_iota(jnp.int32, (t, t), 0)
    col = lax.broadcasted_iota(jnp.int32, (t, t), 1)
    earlier = col < row
    tri = tri_ref[...]
    zero_q = jnp.zeros_like(q)
    qs = (jnp.where(lane < HEAD_DIM, q, zero_q), jnp.where(lane >= HEAD_DIM, q, zero_q))
    kam = kam_ref[0:1, :]
    zmax = tuple(_score_bound(qh, kam) for qh in qs)

    def block(kb, state, diag):
        start = pl.multiple_of(kb * t, t)
        kblk = k_ref[0, pl.ds(start, t), :]
        vblk = v_ref[0, pl.ds(start, t), :]
        new = []
        for h in range(2):
            carry, acc = state[2 * h], state[2 * h + 1]
            z = _dot_nt(qs[h], kblk)
            sp = _softplus(z)
            if diag:
                sp = jnp.where(earlier, sp, 0.0)
            hi, lo = _split2(sp)
            cum = _dot(hi, tri) + _dot(lo, tri)
            w = jnp.exp(z - cum - carry)
            if diag:
                w = jnp.where(earlier, w, 0.0)
            acc = acc + _dot(w.astype(BF16), vblk)
            carry = carry + cum[:, 0:1]
            new += [carry, acc]
        return tuple(new)

    def live(state):
        worst = jnp.maximum(jnp.max(zmax[0] - state[0]), jnp.max(zmax[1] - state[2]))
        return worst > -ZERO_EXP

    init = (jnp.zeros((t, 1), F32), jnp.zeros((t, LANES), F32)) * 2
    state = block(qi, init, True)

    def cond(c):
        return (c[0] >= 0) & c[1]

    def body(c):
        state = block(c[0], c[2:], False)
        return (c[0] - 1, live(state)) + state

    c = lax.while_loop(cond, body, (qi - 1, live(state)) + state)
    o_ref[0] = jnp.where(lane < HEAD_DIM, c[3], c[5]).astype(o_ref.dtype)


def _att_tri():
    t = ATT_BLOCK
    s = lax.broadcasted_iota(jnp.int32, (t, t), 0)
    j = lax.broadcasted_iota(jnp.int32, (t, t), 1)
    return (s >= j).astype(BF16)


def _sb_call(qkv):
    b, seq, _ = qkv.shape
    t = ATT_BLOCK
    pairs = SB_WIDTH // LANES
    q_off, k_off, v_off = 0, SB_WIDTH // LANES, 2 * SB_WIDTH // LANES
    return pl.pallas_call(
        _sb_kernel,
        grid=(b, pairs, seq // t),
        in_specs=[
            pl.BlockSpec((1, t, LANES), lambda bi, hp, qi: (bi, qi, q_off + hp)),
            pl.BlockSpec((1, seq, LANES), lambda bi, hp, qi: (bi, 0, k_off + hp)),
            pl.BlockSpec((1, seq, LANES), lambda bi, hp, qi: (bi, 0, v_off + hp)),
            pl.BlockSpec((t, t), lambda bi, hp, qi: (0, 0)),
        ],
        out_specs=pl.BlockSpec((1, t, LANES), lambda bi, hp, qi: (bi, qi, hp)),
        out_shape=jax.ShapeDtypeStruct((b, seq, SB_WIDTH), BF16),
        scratch_shapes=[pltpu.VMEM((8, LANES), F32)],
        compiler_params=_cparams("arbitrary", "arbitrary", "arbitrary"),
        name="sb_attn",
    )(qkv, qkv, qkv, _att_tri())


def _fox_kernel(q_ref, k_ref, v_ref, ck_ref, o_ref, kam_ref):
    t = ATT_BLOCK
    qi = pl.program_id(2)

    @pl.when(qi == 0)
    def _():
        _key_absmax(k_ref, kam_ref)

    q = q_ref[0] * jnp.asarray(SCALE, BF16)
    lane = lax.broadcasted_iota(jnp.int32, (t, LANES), 1)
    row = lax.broadcasted_iota(jnp.int32, (t, t), 0)
    col = lax.broadcasted_iota(jnp.int32, (t, t), 1)
    visible = col <= row
    own = (lane < HEAD_DIM, lane >= HEAD_DIM)
    zero_q = jnp.zeros_like(q)
    qs = tuple(jnp.where(o, q, zero_q) for o in own)
    kam = kam_ref[0:1, :]
    zmax = tuple(_score_bound(qh, kam) for qh in qs)

    def block(kb, state, diag):
        start = pl.multiple_of(kb * t, t)
        kblk = k_ref[0, pl.ds(start, t), :]
        vblk = v_ref[0, pl.ds(start, t), :]
        one_v = jnp.ones_like(vblk)
        new = []
        for h in range(2):
            m, acc = state[2 * h], state[2 * h + 1]
            ck = ck_ref[0, h, pl.ds(kb, 1), :]
            s = _dot_nt(qs[h], kblk) - ck
            if diag:
                s = jnp.where(visible, s, -jnp.inf)
            m_new = jnp.maximum(m, jnp.max(s, axis=-1, keepdims=True))
            p = jnp.exp(s - m_new)
            vmod = jnp.where(own[h], vblk, one_v)
            acc = jnp.exp(m - m_new) * acc + _dot(p.astype(BF16), vmod)
            new += [m_new, acc]
        return tuple(new)

    def live(kb, state):
        kbc = jnp.maximum(kb, 0)
        worst = None
        for h in range(2):
            ck_min = jnp.min(ck_ref[0, h, pl.ds(kbc, 1), :], axis=-1, keepdims=True)
            w_h = jnp.max(zmax[h] - ck_min - state[2 * h])
            worst = w_h if worst is None else jnp.maximum(worst, w_h)
        return worst > -ZERO_EXP

    init = (jnp.full((t, 1), -jnp.inf, F32), jnp.zeros((t, LANES), F32)) * 2
    state = block(qi, init, True)

    def cond(c):
        return (c[0] >= 0) & c[1]

    def body(c):
        state = block(c[0], c[2:], False)
        return (c[0] - 1, live(c[0] - 1, state)) + state

    c = lax.while_loop(cond, body, (qi - 1, live(qi - 1, state)) + state)
    outs = [a / pltpu.roll(a, HEAD_DIM, 1) for a in (c[3], c[5])]
    o_ref[0] = jnp.where(own[0], outs[0], outs[1]).astype(o_ref.dtype)


def _fox_call(qkv, cf4):
    b, seq, _ = qkv.shape
    t = ATT_BLOCK
    pairs = FOX_WIDTH // LANES
    base = 3 * SB_WIDTH // LANES
    q_off, k_off, v_off = base, base + pairs, base + 2 * pairs
    return pl.pallas_call(
        _fox_kernel,
        grid=(b, pairs, seq // t),
        in_specs=[
            pl.BlockSpec((1, t, LANES), lambda bi, hp, qi: (bi, qi, q_off + hp)),
            pl.BlockSpec((1, seq, LANES), lambda bi, hp, qi: (bi, 0, k_off + hp)),
            pl.BlockSpec((1, seq, LANES), lambda bi, hp, qi: (bi, 0, v_off + hp)),
            pl.BlockSpec((1, 2, seq // t, t), lambda bi, hp, qi: (bi, hp, 0, 0)),
        ],
        out_specs=pl.BlockSpec((1, t, LANES), lambda bi, hp, qi: (bi, qi, hp)),
        out_shape=jax.ShapeDtypeStruct((b, seq, FOX_WIDTH), BF16),
        scratch_shapes=[pltpu.VMEM((8, LANES), F32)],
        compiler_params=_cparams("arbitrary", "arbitrary", "arbitrary"),
        name="fox_attn",
    )(qkv, qkv, qkv, cf4)


def _post_kernel(sb_ref, fx_ref, x_ref, p_ref, wo_ref, g1_ref, b1_ref, wr_ref, br_ref, wpg_ref,
                 wp_ref, tri_ref, base_ref, h_ref, route_ref, cnt_ref, carry_ref, *, alpha):
    i = pl.program_id(0)

    @pl.when(i == 0)
    def _():
        carry_ref[...] = jnp.zeros_like(carry_ref)

    mix = _dot(sb_ref[...], wo_ref[:SB_WIDTH, :]) + _dot(fx_ref[...], wo_ref[SB_WIDTH:, :])
    h = _layer_norm(alpha * x_ref[...] + mix, g1_ref[...], b1_ref[...])
    h_ref[...] = h

    tm = h.shape[0]
    lane = lax.broadcasted_iota(jnp.int32, (tm, LANES), 1)
    big = jnp.int32(4 * LANES)
    logits = jnp.dot(h, wr_ref[...], preferred_element_type=F32,
                     precision=lax.Precision.HIGHEST) + br_ref[...]

    def first_argmax(vals):
        vmax = jnp.max(vals, axis=-1, keepdims=True)
        idx = jnp.min(jnp.where(vals == vmax, lane, big), axis=-1, keepdims=True)
        return vmax, idx

    neg = -jnp.inf
    gl = jnp.where(lane < N_GROUPS, logits, neg)
    gmax, gsel = first_argmax(gl)
    g_prob = 1.0 / jnp.sum(jnp.exp(gl - gmax), axis=-1, keepdims=True)
    lo = ROUTE_OFF + gsel * EXPERTS_PER_GROUP
    el = jnp.where((lane >= lo) & (lane < lo + EXPERTS_PER_GROUP), logits, neg)
    v1, i1 = first_argmax(el)
    v2, i2 = first_argmax(jnp.where(lane == i1, neg, el))
    e2 = jnp.exp(v2 - v1)
    gate1 = g_prob / (1.0 + e2)
    gate2 = g_prob * e2 / (1.0 + e2)

    hit1 = lane == i1
    hit2 = lane == i2
    onehot = jnp.where(hit1 | hit2, 1.0, 0.0)
    before = _dot(tri_ref[...], onehot.astype(BF16)) + carry_ref[...]
    rank1 = jnp.sum(jnp.where(hit1, before, 0.0), axis=-1, keepdims=True)
    rank2 = jnp.sum(jnp.where(hit2, before, 0.0), axis=-1, keepdims=True)
    total = before[-1:, :] + onehot[-1:, :]
    carry_ref[...] = total
    cnt_ref[...] = jnp.broadcast_to(total, cnt_ref.shape)

    route = jnp.zeros((tm, LANES), F32)
    fields = ((i1 - ROUTE_OFF).astype(F32), (i2 - ROUTE_OFF).astype(F32), gate1, gate2, rank1, rank2)
    for k, val in enumerate(fields):
        route = jnp.where(lane == k, val, route)
    route_ref[...] = route

    ple = _dot(p_ref[...].astype(BF16), wp_ref[...])
    gate = jax.nn.sigmoid(_dot(h.astype(BF16), wpg_ref[...]))
    base_ref[...] = alpha * h + ple * gate


def _post_call(sb, fx, x2, p2, wo, g1, b1, wr, br, wpg, wp, alpha):
    n, d = x2.shape
    tm = min(POST_TILE, n)
    row = lax.broadcasted_iota(jnp.int32, (tm, tm), 0)
    col = lax.broadcasted_iota(jnp.int32, (tm, tm), 1)
    tri = (col < row).astype(BF16)
    const = lambda i: (0, 0)
    tile = lambda i: (i, 0)
    return pl.pallas_call(
        functools.partial(_post_kernel, alpha=alpha),
        grid=(n // tm,),
        in_specs=[
            pl.BlockSpec((tm, SB_WIDTH), tile),
            pl.BlockSpec((tm, FOX_WIDTH), tile),
            pl.BlockSpec((tm, d), tile),
            pl.BlockSpec((tm, p2.shape[1]), tile),
            pl.BlockSpec(wo.shape, const),
            pl.BlockSpec((1, d), const),
            pl.BlockSpec((1, d), const),
            pl.BlockSpec(wr.shape, const),
            pl.BlockSpec((1, LANES), const),
            pl.BlockSpec(wpg.shape, const),
            pl.BlockSpec(wp.shape, const),
            pl.BlockSpec((tm, tm), const),
        ],
        out_specs=[
            pl.BlockSpec((tm, d), tile),
            pl.BlockSpec((tm, d), tile),
            pl.BlockSpec((tm, LANES), tile),
            pl.BlockSpec((8, LANES), const),
        ],
        out_shape=[
            jax.ShapeDtypeStruct((n, d), F32),
            jax.ShapeDtypeStruct((n, d), F32),
            jax.ShapeDtypeStruct((n, LANES), F32),
            jax.ShapeDtypeStruct((8, LANES), F32),
        ],
        scratch_shapes=[pltpu.VMEM((1, LANES), F32)],
        compiler_params=_cparams("arbitrary"),
        name="post_attn",
    )(sb, fx, x2, p2, wo, g1, b1, wr, br, wpg, wp, tri)


def _dispatch_kernel(dest_ref, h_ref, xs_in_ref, xs_ref, sem):
    del xs_in_ref
    tt = h_ref.shape[0]

    def copy(r, k):
        return pltpu.make_async_copy(h_ref.at[pl.ds(r, 1), :],
                                     xs_ref.at[pl.ds(dest_ref[TOP_K * r + k], 1), :], sem)

    def start(r, c):
        for k in range(TOP_K):
            copy(r, k).start()
        return c

    def wait(r, c):
        for k in range(TOP_K):
            copy(r, k).wait()
        return c

    lax.fori_loop(0, tt, start, 0)
    lax.fori_loop(0, tt, wait, 0)


def _dispatch_call(h, dest_flat, xs_zero):
    n, d = h.shape
    tt = min(MOVE_TILE, n)
    return pl.pallas_call(
        _dispatch_kernel,
        grid=(n // tt,),
        in_specs=[
            pl.BlockSpec((TOP_K * tt,), lambda i: (i,), memory_space=pltpu.SMEM),
            pl.BlockSpec((tt, d), lambda i: (i, 0)),
            pl.BlockSpec(memory_space=pl.ANY),
        ],
        out_specs=pl.BlockSpec(memory_space=pl.ANY),
        out_shape=jax.ShapeDtypeStruct(xs_zero.shape, xs_zero.dtype),
        scratch_shapes=[pltpu.SemaphoreType.DMA(())],
        input_output_aliases={2: 0},
        compiler_params=_cparams("arbitrary"),
        name="moe_dispatch",
    )(dest_flat, h, xs_zero)


def _expert_kernel(be_ref, nb_ref, xs_ref, wg_ref, wu_ref, wd_ref, ys_ref):
    i = pl.program_id(0)

    @pl.when(i < nb_ref[0])
    def _():
        xb = xs_ref[...].astype(BF16)
        g = _dot(xb, wg_ref[0])
        u = _dot(xb, wu_ref[0])
        hdn = (g * jax.nn.sigmoid(g)) * u
        ys_ref[...] = _dot(hdn.astype(BF16), wd_ref[0])

    @pl.when(i >= nb_ref[0])
    def _():
        ys_ref[...] = jnp.zeros_like(ys_ref)


def _expert_call(blk_expert, n_used, xs, wg, wu, wd):
    cap, d = xs.shape
    f = wg.shape[2]
    nblk = cap // MOE_BLOCK
    return pl.pallas_call(
        _expert_kernel,
        grid_spec=pltpu.PrefetchScalarGridSpec(
            num_scalar_prefetch=2,
            grid=(nblk,),
            in_specs=[
                pl.BlockSpec((MOE_BLOCK, d), lambda i, be, nb: (i, 0)),
                pl.BlockSpec((1, d, f), lambda i, be, nb: (be[i], 0, 0)),
                pl.BlockSpec((1, d, f), lambda i, be, nb: (be[i], 0, 0)),
                pl.BlockSpec((1, f, d), lambda i, be, nb: (be[i], 0, 0)),
            ],
            out_specs=pl.BlockSpec((MOE_BLOCK, d), lambda i, be, nb: (i, 0)),
        ),
        out_shape=jax.ShapeDtypeStruct((cap, d), F32),
        compiler_params=_cparams("arbitrary"),
        name="moe_experts",
    )(blk_expert, n_used, xs, wg, wu, wd)


def _combine_kernel(dest_ref, base_ref, route_ref, g2_ref, b2_ref, ys_ref, o_ref, buf_ref, sem):
    tt = base_ref.shape[0]

    def copy(r, k):
        return pltpu.make_async_copy(ys_ref.at[pl.ds(dest_ref[TOP_K * r + k], 1), :],
                                     buf_ref.at[k, pl.ds(r, 1), :], sem)

    def start(r, c):
        for k in range(TOP_K):
            copy(r, k).start()
        return c

    def wait(r, c):
        for k in range(TOP_K):
            copy(r, k).wait()
        return c

    lax.fori_loop(0, tt, start, 0)
    lax.fori_loop(0, tt, wait, 0)

    route = route_ref[...]
    ffn = route[:, 2:3] * buf_ref[0] + route[:, 3:4] * buf_ref[1]
    o_ref[...] = _layer_norm(base_ref[...] + ffn, g2_ref[...], b2_ref[...])


def _combine_call(dest_flat, base, route, g2, b2, ys):
    n, d = base.shape
    tt = min(MOVE_TILE, n)
    return pl.pallas_call(
        _combine_kernel,
        grid=(n // tt,),
        in_specs=[
            pl.BlockSpec((TOP_K * tt,), lambda i: (i,), memory_space=pltpu.SMEM),
            pl.BlockSpec((tt, d), lambda i: (i, 0)),
            pl.BlockSpec((tt, LANES), lambda i: (i, 0)),
            pl.BlockSpec((1, d), lambda i: (0, 0)),
            pl.BlockSpec((1, d), lambda i: (0, 0)),
            pl.BlockSpec(memory_space=pl.ANY),
        ],
        out_specs=pl.BlockSpec((tt, d), lambda i: (i, 0)),
        out_shape=jax.ShapeDtypeStruct((n, d), F32),
        scratch_shapes=[pltpu.VMEM((TOP_K, tt, d), F32), pltpu.SemaphoreType.DMA(())],
        compiler_params=_cparams("arbitrary"),
        name="moe_combine",
    )(dest_flat, base, route, g2, b2, ys)


def _layer(h, p_i, w_in, b_forget, w_out, g1, b1, w_group, b_group, w_router, b_router,
           w_gate, w_up, w_down, w_ple, w_ple_gate, g2, b2, alpha):
    bsz, seq, d = h.shape
    n = bsz * seq
    x2 = h.reshape(n, d)

    in_cols = w_in.shape[1]
    w_pad = jnp.pad(w_in, ((0, 0), (0, QKV_COLS + LANES - in_cols))).astype(BF16)
    bf_pad = jnp.pad(b_forget.astype(F32), (0, LANES - FOX_HEADS)).reshape(1, LANES)
    wr = jnp.pad(jnp.concatenate([w_group, w_router], axis=1).astype(F32),
                 ((0, 0), (0, LANES - N_GROUPS - N_EXPERTS)))
    br = jnp.pad(jnp.concatenate([b_group, b_router]).astype(F32),
                 (0, LANES - N_GROUPS - N_EXPERTS)).reshape(1, LANES)

    qkv, cf = _proj_call(x2, w_pad, bf_pad, seq)
    qkv3 = qkv.reshape(bsz, seq, QKV_COLS)
    cf4 = cf[:, :FOX_HEADS].reshape(bsz, seq, FOX_HEADS).transpose(0, 2, 1)
    cf4 = cf4.reshape(bsz, FOX_HEADS, seq // ATT_BLOCK, ATT_BLOCK)

    sb = _sb_call(qkv3).reshape(n, SB_WIDTH)
    fx = _fox_call(qkv3, cf4).reshape(n, FOX_WIDTH)

    base, h1, route, counts = _post_call(
        sb, fx, x2, p_i.reshape(n, -1), w_out.astype(BF16), g1.reshape(1, d), b1.reshape(1, d),
        wr, br, w_ple_gate.astype(BF16), w_ple.astype(BF16), alpha)

    counts = counts[0, ROUTE_OFF:ROUTE_OFF + N_EXPERTS].astype(jnp.int32)
    padded = ((counts + MOE_BLOCK - 1) // MOE_BLOCK) * MOE_BLOCK
    pad_end = jnp.cumsum(padded)
    pad_start = pad_end - padded
    eid = route[:, 0:TOP_K].astype(jnp.int32)
    rank = route[:, 4:4 + TOP_K].astype(jnp.int32)
    dest = (pad_start[eid] + rank).reshape(n * TOP_K)
    cap = n * TOP_K + N_EXPERTS * MOE_BLOCK
    nblk = cap // MOE_BLOCK
    blk_expert = jnp.minimum(
        jnp.searchsorted(pad_end, jnp.arange(nblk, dtype=jnp.int32) * MOE_BLOCK, side='right'),
        N_EXPERTS - 1).astype(jnp.int32)
    n_used = (pad_end[-1:] // MOE_BLOCK).astype(jnp.int32)

    xs = _dispatch_call(h1, dest, jnp.zeros((cap, d), F32))
    ys = _expert_call(blk_expert, n_used, xs, w_gate.astype(BF16), w_up.astype(BF16),
                      w_down.astype(BF16))
    out = _combine_call(dest, base, route, g2.reshape(1, d), b2.reshape(1, d), ys)
    return out.reshape(bsz, seq, d)


def kernel(x, p, w_in, b_forget, w_out, ln_mix_g, ln_mix_b, w_group, b_group, w_router, b_router,
           w_gate, w_up, w_down, w_ple, w_ple_gate, ln_ffn_g, ln_ffn_b):
    depth = w_in.shape[0]
    alpha = (2 * depth) ** 0.25
    h = x
    for i in range(depth):
        h = _layer(h, p[i], w_in[i], b_forget[i], w_out[i], ln_mix_g[i], ln_mix_b[i], w_group[i],
                   b_group[i], w_router[i], b_router[i], w_gate[i], w_up[i], w_down[i], w_ple[i],
                   w_ple_gate[i], ln_ffn_g[i], ln_ffn_b[i], alpha)
    return h
```

```python
import functools

import jax
import jax.numpy as jnp
from jax import lax
from jax.experimental import pallas as pl
from jax.experimental.pallas import tpu as pltpu

F32 = jnp.float32
BF16 = jnp.bfloat16

HEAD_DIM = 64
SB_HEADS = 8
FOX_HEADS = 8
SB_WIDTH = SB_HEADS * HEAD_DIM
FOX_WIDTH = FOX_HEADS * HEAD_DIM
N_GROUPS = 4
EXPERTS_PER_GROUP = 8
N_EXPERTS = N_GROUPS * EXPERTS_PER_GROUP
TOP_K = 2
LN_EPS = 1e-5
SCALE = HEAD_DIM ** -0.5

LANES = 128
ATT_BLOCK = 256
ROW_TILE = 512
POST_TILE = 256
MOE_BLOCK = 256
MOVE_TILE = 256
DISPATCH_TILE = 512
ROUTE_OFF = N_GROUPS
VMEM_LIMIT = 56 * 1024 * 1024


def _cparams(*sem):
    return pltpu.CompilerParams(dimension_semantics=sem, vmem_limit_bytes=VMEM_LIMIT)


def _dot(a, b):
    return jnp.dot(a, b, preferred_element_type=F32)


def _dot_nt(a, b):
    return lax.dot_general(a, b, (((1,), (1,)), ((), ())), preferred_element_type=F32)


def _split2(x):
    hi = x.astype(BF16)
    lo = (x - hi.astype(F32)).astype(BF16)
    return hi, lo


def _split3(x):
    hi = x.astype(BF16)
    r = x - hi.astype(F32)
    mid = r.astype(BF16)
    lo = (r - mid.astype(F32)).astype(BF16)
    return hi, mid, lo


def _layer_norm(u, g, b):
    mu = jnp.mean(u, axis=-1, keepdims=True)
    d = u - mu
    var = jnp.mean(d * d, axis=-1, keepdims=True)
    return d * lax.rsqrt(var + LN_EPS) * g + b


SB_COLS = 3 * SB_WIDTH
FXK_OFF = SB_COLS + FOX_WIDTH
FGATE_OFF = FXK_OFF + FOX_HEADS * LANES
PROJ_COLS = FGATE_OFF + LANES
AUG_LANE = HEAD_DIM
AUG_TERMS = 3


def _proj_kernel(x_ref, w_ref, wvt_ref, bf_ref, tri_ref, sel_ref, qsb_ref, qfx_ref, kaug_ref,
                 vt_ref, cfmin_ref, carry_ref, *, tiles_per_seq):
    i = pl.program_id(0)
    t = ATT_BLOCK

    @pl.when(i % tiles_per_seq == 0)
    def _():
        carry_ref[...] = jnp.zeros_like(carry_ref)

    xb = x_ref[...].astype(BF16)
    tm = xb.shape[0]
    chunk = 512
    for c in range(SB_COLS // chunk):
        sl = slice(c * chunk, (c + 1) * chunk)
        qsb_ref[:, sl] = _dot(xb, w_ref[:, sl]).astype(BF16)
    qfx_ref[...] = _dot(xb, w_ref[:, SB_COLS:FXK_OFF]).astype(BF16)

    fl = _dot(xb, w_ref[:, FGATE_OFF:]) + bf_ref[...]
    lf = jnp.minimum(fl, 0.0) - jnp.log(1.0 + jnp.exp(-jnp.abs(fl)))
    tri = tri_ref[...]
    hi, mid, lo = _split3(lf)
    cum = _dot(tri, hi) + _dot(tri, mid) + _dot(tri, lo) + carry_ref[...]
    carry_ref[...] = cum[-1:, :]

    nhi, nmid, nlo = _split3(-cum)
    lane = lax.broadcasted_iota(jnp.int32, (tm, LANES), 1)
    terms = jnp.where(lane < FOX_HEADS, nhi, jnp.where(lane < 2 * FOX_HEADS, nmid, nlo))
    aug = _dot(terms, sel_ref[...])
    for c in range((FGATE_OFF - FXK_OFF) // chunk):
        sl = slice(c * chunk, (c + 1) * chunk)
        wsl = slice(FXK_OFF + c * chunk, FXK_OFF + (c + 1) * chunk)
        kaug_ref[:, sl] = (_dot(xb, w_ref[:, wsl]) + aug[:, sl]).astype(BF16)

    mins = []
    for c in range(tm // t):
        rows = slice(c * t, (c + 1) * t)
        vt_ref[c] = _dot_nt(wvt_ref[...], xb[rows, :]).astype(BF16)
        mins.append(jnp.min(cum[rows, :], axis=0, keepdims=True))
    mins.append(jnp.zeros((8 - tm // t, LANES), F32))
    cfmin_ref[0] = jnp.concatenate(mins, axis=0)


def _proj_weights(w_in, b_forget):
    d = w_in.shape[0]
    off_q_fx = SB_COLS
    off_k_fx = off_q_fx + FOX_WIDTH
    off_v_fx = off_k_fx + FOX_WIDTH
    off_f = off_v_fx + FOX_WIDTH
    wk = w_in[:, off_k_fx:off_v_fx].reshape(d, FOX_HEADS, HEAD_DIM)
    wk = jnp.pad(wk, ((0, 0), (0, 0), (0, LANES - HEAD_DIM))).reshape(d, FOX_HEADS * LANES)
    wf = jnp.tile(w_in[:, off_f:off_f + FOX_HEADS], (1, AUG_TERMS))
    wf = jnp.pad(wf, ((0, 0), (0, LANES - AUG_TERMS * FOX_HEADS)))
    w_pad = jnp.concatenate([w_in[:, :off_k_fx], wk, wf], axis=1).astype(BF16)
    wvt = w_in[:, off_v_fx:off_f].T.astype(BF16)
    bf = jnp.pad(jnp.tile(b_forget.astype(F32), AUG_TERMS), (0, LANES - AUG_TERMS * FOX_HEADS))
    src = lax.broadcasted_iota(jnp.int32, (LANES, FOX_HEADS * LANES), 0)
    dst = lax.broadcasted_iota(jnp.int32, (LANES, FOX_HEADS * LANES), 1)
    head, term = src % FOX_HEADS, src // FOX_HEADS
    sel = ((term < AUG_TERMS) & (dst == head * LANES + AUG_LANE + term)).astype(BF16)
    return w_pad, wvt, bf.reshape(1, LANES), sel


def _proj_call(x2, w_in, b_forget, seq):
    n, d = x2.shape
    tm = min(ROW_TILE, seq)
    t = ATT_BLOCK
    w_pad, wvt, bf_pad, sel = _proj_weights(w_in, b_forget)
    row = lax.broadcasted_iota(jnp.int32, (tm, tm), 0)
    col = lax.broadcasted_iota(jnp.int32, (tm, tm), 1)
    tri = (col <= row).astype(BF16)
    const = lambda i: (0, 0)
    tile = lambda i: (i, 0)
    return pl.pallas_call(
        functools.partial(_proj_kernel, tiles_per_seq=seq // tm),
        grid=(n // tm,),
        in_specs=[
            pl.BlockSpec((tm, d), tile),
            pl.BlockSpec((d, PROJ_COLS), const),
            pl.BlockSpec((FOX_WIDTH, d), const),
            pl.BlockSpec((1, LANES), const),
            pl.BlockSpec((tm, tm), const),
            pl.BlockSpec(sel.shape, const),
        ],
        out_specs=[
            pl.BlockSpec((tm, SB_COLS), tile),
            pl.BlockSpec((tm, FOX_WIDTH), tile),
            pl.BlockSpec((tm, FOX_HEADS * LANES), tile),
            pl.BlockSpec((tm // t, FOX_WIDTH, t), lambda i: (i, 0, 0)),
            pl.BlockSpec((1, 8, LANES), lambda i: (i, 0, 0)),
        ],
        out_shape=[
            jax.ShapeDtypeStruct((n, SB_COLS), BF16),
            jax.ShapeDtypeStruct((n, FOX_WIDTH), BF16),
            jax.ShapeDtypeStruct((n, FOX_HEADS * LANES), BF16),
            jax.ShapeDtypeStruct((n // t, FOX_WIDTH, t), BF16),
            jax.ShapeDtypeStruct((n // tm, 8, LANES), F32),
        ],
        scratch_shapes=[pltpu.VMEM((1, LANES), F32)],
        compiler_params=_cparams("arbitrary"),
        name="proj",
    )(x2, w_pad, wvt, bf_pad, tri, sel)


ZERO_EXP = 110.0


def _softplus(z):
    return jnp.maximum(z, 0.0) + jnp.log(1.0 + jnp.exp(-jnp.abs(z)))


def _key_absmax(k_ref, kam_ref):
    seq, width = k_ref.shape[1], k_ref.shape[2]
    rows = min(512, seq)

    def body(i, m):
        blk = k_ref[0, pl.ds(pl.multiple_of(i * rows, rows), rows), :]
        return jnp.maximum(m, jnp.max(jnp.abs(blk.astype(F32)), axis=0, keepdims=True))

    kam = lax.fori_loop(0, seq // rows, body, jnp.zeros((1, width), F32))
    kam_ref[...] = jnp.broadcast_to(kam, kam_ref.shape)


def _score_bound(qh, kam):
    zmax = jnp.sum(jnp.abs(qh.astype(F32)) * kam, axis=-1, keepdims=True)
    return zmax * (1.0 + 2.0 ** -10) + 2.0 ** -10


def _sb_kernel(q_ref, k_ref, v_ref, tri_ref, o_ref, kam_ref):
    t = ATT_BLOCK
    qi = pl.program_id(2)

    @pl.when(qi == 0)
    def _():
        _key_absmax(k_ref, kam_ref)

    q = q_ref[0] * jnp.asarray(SCALE, BF16)
    lane = lax.broadcasted_iota(jnp.int32, (t, LANES), 1)
    row = lax.broadcasted_iota(jnp.int32, (t, t), 0)
    col = lax.broadcasted_iota(jnp.int32, (t, t), 1)
    earlier = col < row
    tri = tri_ref[...]
    zero_q = jnp.zeros_like(q)
    qs = (jnp.where(lane < HEAD_DIM, q, zero_q), jnp.where(lane >= HEAD_DIM, q, zero_q))
    kam = kam_ref[0:1, :]
    zmax = tuple(_score_bound(qh, kam) for qh in qs)

    def block(kb, state, diag):
        start = pl.multiple_of(kb * t, t)
        kblk = k_ref[0, pl.ds(start, t), :]
        vblk = v_ref[0, pl.ds(start, t), :]
        new = []
        for h in range(2):
            carry, acc = state[2 * h], state[2 * h + 1]
            z = _dot_nt(qs[h], kblk)
            sp = _softplus(z)
            if diag:
                sp = jnp.where(earlier, sp, 0.0)
            hi, lo = _split2(sp)
            cum = _dot(hi, tri) + _dot(lo, tri)
            w = jnp.exp(z - cum - carry)
            if diag:
                w = jnp.where(earlier, w, 0.0)
            acc = acc + _dot(w.astype(BF16), vblk)
            carry = carry + cum[:, 0:1]
            new += [carry, acc]
        return tuple(new)

    def live(state):
        worst = jnp.maximum(jnp.max(zmax[0] - state[0]), jnp.max(zmax[1] - state[2]))
        return worst > -ZERO_EXP

    init = (jnp.zeros((t, 1), F32), jnp.zeros((t, LANES), F32)) * 2
    state = block(qi, init, True)

    def cond(c):
        return (c[0] >= 0) & c[1]

    def body(c):
        state = block(c[0], c[2:], False)
        return (c[0] - 1, live(state)) + state

    c = lax.while_loop(cond, body, (qi - 1, live(state)) + state)
    o_ref[0] = jnp.where(lane < HEAD_DIM, c[3], c[5]).astype(o_ref.dtype)


def _att_tri():
    t = ATT_BLOCK
    s = lax.broadcasted_iota(jnp.int32, (t, t), 0)
    j = lax.broadcasted_iota(jnp.int32, (t, t), 1)
    return (s >= j).astype(BF16)


def _sb_call(qkv):
    b, seq, _ = qkv.shape
    t = ATT_BLOCK
    pairs = SB_WIDTH // LANES
    q_off, k_off, v_off = 0, SB_WIDTH // LANES, 2 * SB_WIDTH // LANES
    return pl.pallas_call(
        _sb_kernel,
        grid=(b, pairs, seq // t),
        in_specs=[
            pl.BlockSpec((1, t, LANES), lambda bi, hp, qi: (bi, qi, q_off + hp)),
            pl.BlockSpec((1, seq, LANES), lambda bi, hp, qi: (bi, 0, k_off + hp)),
            pl.BlockSpec((1, seq, LANES), lambda bi, hp, qi: (bi, 0, v_off + hp)),
            pl.BlockSpec((t, t), lambda bi, hp, qi: (0, 0)),
        ],
        out_specs=pl.BlockSpec((1, t, LANES), lambda bi, hp, qi: (bi, qi, hp)),
        out_shape=jax.ShapeDtypeStruct((b, seq, SB_WIDTH), BF16),
        scratch_shapes=[pltpu.VMEM((8, LANES), F32)],
        compiler_params=_cparams("arbitrary", "arbitrary", "arbitrary"),
        name="sb_attn",
    )(qkv, qkv, qkv, _att_tri())


def _fox_kernel(cfmin_ref, q_ref, k_ref, vt_ref, o_ref, s_scr, p_scr, kam_ref, *, nblk):
    t = ATT_BLOCK
    bi, hp, qi = pl.program_id(0), pl.program_id(1), pl.program_id(2)

    @pl.when(qi == 0)
    def _():
        _key_absmax(k_ref, kam_ref)

    lane = lax.broadcasted_iota(jnp.int32, (t, LANES), 1)
    feat = lax.broadcasted_iota(jnp.int32, (LANES, t), 0)
    key = lax.broadcasted_iota(jnp.int32, (t, t), 0)
    qry = lax.broadcasted_iota(jnp.int32, (t, t), 1)
    visible = key <= qry
    qf = q_ref[0].astype(F32) * SCALE
    ones_aug = jnp.where(lane < AUG_LANE + AUG_TERMS, 1.0, 0.0)

    qt, zmax, own_rows = [], [], []
    for h in range(2):
        qh = qf if h == 0 else pltpu.roll(qf, HEAD_DIM, 1)
        qht = jnp.where(lane < HEAD_DIM, qh, ones_aug).T
        qt.append(qht.astype(BF16))
        kam = kam_ref[0:1, h * LANES:(h + 1) * LANES] * (1.0 + 2.0 ** -7)
        kam16 = jnp.broadcast_to(jnp.where(lane[0:1, :] < HEAD_DIM, kam, 0.0), (16, LANES)).astype(BF16)
        aq = jnp.where(feat < HEAD_DIM, jnp.abs(qht), 0.0).astype(BF16)
        zmax.append(_dot(kam16, aq)[0:1, :] * (1.0 + 2.0 ** -10) + 2.0 ** -10)
        own_rows.append((feat < HEAD_DIM) if h == 0 else (feat >= HEAD_DIM))

    def qk(kb, h):
        start = pl.multiple_of(kb * t, t)
        return _dot(k_ref[0, pl.ds(start, t), h * LANES:(h + 1) * LANES], qt[h])

    def softmax_part(s, m):
        m_new = jnp.maximum(m, jnp.max(s, axis=0, keepdims=True))
        return m_new, jnp.exp(s - m_new).astype(BF16), jnp.exp(m - m_new)

    def pv(kb, h, alpha, acc):
        vt = vt_ref[kb]
        vmod = jnp.where(own_rows[h], vt, jnp.ones_like(vt))
        return alpha * acc + _dot(vmod, p_scr[h])

    def live(kb, ms):
        worst = None
        for h in range(2):
            cf_min = cfmin_ref[(bi * FOX_HEADS + 2 * hp + h) * nblk + kb]
            w_h = jnp.max(zmax[h] - ms[h]) - cf_min
            worst = w_h if worst is None else jnp.maximum(worst, w_h)
        return worst > -ZERO_EXP

    prev = jnp.maximum(qi - 1, 0)
    ms, alphas = [], []
    for h in range(2):
        s = jnp.where(visible, qk(qi, h), -jnp.inf)
        m, p, alpha = softmax_part(s, jnp.full((1, t), -jnp.inf, F32))
        p_scr[h] = p
        s_scr[h] = qk(prev, h)
        ms.append(m)
        alphas.append(alpha)
    accs = [jnp.zeros((LANES, t), F32)] * 2

    def cond(c):
        return (c[0] <= qi) & c[1]

    def body(c):
        j = c[0]
        ms, alphas, accs = list(c[2:4]), list(c[4:6]), list(c[6:8])
        nxt = jnp.maximum(qi - j - 1, 0)
        for h in range(2):
            accs[h] = pv(qi - j + 1, h, alphas[h], accs[h])
        for h in range(2):
            ms[h], p, alphas[h] = softmax_part(s_scr[h], ms[h])
            p_scr[h] = p
        for h in range(2):
            s_scr[h] = qk(nxt, h)
        return (j + 1, live(nxt, ms)) + tuple(ms) + tuple(alphas) + tuple(accs)

    c = lax.while_loop(cond, body, (jnp.int32(1), live(prev, ms)) + tuple(ms) + tuple(alphas) + tuple(accs))
    last = qi - c[0] + 1
    acc0 = pv(last, 0, c[4], c[6])
    acc1 = pv(last, 1, c[5], c[7])
    out_t = jnp.concatenate([acc0[:HEAD_DIM] / acc0[HEAD_DIM:HEAD_DIM + 1],
                             acc1[HEAD_DIM:] / acc1[0:1]], axis=0)
    o_ref[0] = out_t.T.astype(o_ref.dtype)


def _fox_call(qfx, kaug, vt, cfmin):
    b, seq, _ = qfx.shape
    t = ATT_BLOCK
    nblk = seq // t
    pairs = FOX_WIDTH // LANES
    return pl.pallas_call(
        functools.partial(_fox_kernel, nblk=nblk),
        grid=(b, pairs, nblk),
        in_specs=[
            pl.BlockSpec(memory_space=pltpu.SMEM),
            pl.BlockSpec((1, t, LANES), lambda bi, hp, qi: (bi, qi, hp)),
            pl.BlockSpec((1, seq, 2 * LANES), lambda bi, hp, qi: (bi, 0, hp)),
            pl.BlockSpec((nblk, LANES, t), lambda bi, hp, qi: (bi, hp, 0)),
        ],
        out_specs=pl.BlockSpec((1, t, LANES), lambda bi, hp, qi: (bi, qi, hp)),
        out_shape=jax.ShapeDtypeStruct((b, seq, FOX_WIDTH), BF16),
        scratch_shapes=[pltpu.VMEM((2, t, t), F32), pltpu.VMEM((2, t, t), BF16),
                        pltpu.VMEM((8, 2 * LANES), F32)],
        compiler_params=_cparams("arbitrary", "arbitrary", "arbitrary"),
        name="fox_attn",
    )(cfmin, qfx, kaug, vt)


def _post_kernel(sb_ref, fx_ref, x_ref, p_ref, wo_ref, g1_ref, b1_ref, wr_ref, br_ref, wpg_ref,
                 wp_ref, tri_ref, base_ref, h_ref, route_ref, cnt_ref, carry_ref, *, alpha):
    i = pl.program_id(0)

    @pl.when(i == 0)
    def _():
        carry_ref[...] = jnp.zeros_like(carry_ref)

    mix = _dot(sb_ref[...], wo_ref[:SB_WIDTH, :]) + _dot(fx_ref[...], wo_ref[SB_WIDTH:, :])
    h = _layer_norm(alpha * x_ref[...] + mix, g1_ref[...], b1_ref[...])
    h_ref[...] = h

    tm = h.shape[0]
    lane = lax.broadcasted_iota(jnp.int32, (tm, LANES), 1)
    big = jnp.int32(4 * LANES)
    logits = jnp.dot(h, wr_ref[...], preferred_element_type=F32,
                     precision=lax.Precision.HIGHEST) + br_ref[...]

    def first_argmax(vals):
        vmax = jnp.max(vals, axis=-1, keepdims=True)
        idx = jnp.min(jnp.where(vals == vmax, lane, big), axis=-1, keepdims=True)
        return vmax, idx

    neg = -jnp.inf
    gl = jnp.where(lane < N_GROUPS, logits, neg)
    gmax, gsel = first_argmax(gl)
    g_prob = 1.0 / jnp.sum(jnp.exp(gl - gmax), axis=-1, keepdims=True)
    lo = ROUTE_OFF + gsel * EXPERTS_PER_GROUP
    el = jnp.where((lane >= lo) & (lane < lo + EXPERTS_PER_GROUP), logits, neg)
    v1, i1 = first_argmax(el)
    v2, i2 = first_argmax(jnp.where(lane == i1, neg, el))
    e2 = jnp.exp(v2 - v1)
    gate1 = g_prob / (1.0 + e2)
    gate2 = g_prob * e2 / (1.0 + e2)

    hit1 = lane == i1
    hit2 = lane == i2
    onehot = jnp.where(hit1 | hit2, 1.0, 0.0)
    before = _dot(tri_ref[...], onehot.astype(BF16)) + carry_ref[...]
    rank1 = jnp.sum(jnp.where(hit1, before, 0.0), axis=-1, keepdims=True)
    rank2 = jnp.sum(jnp.where(hit2, before, 0.0), axis=-1, keepdims=True)
    total = before[-1:, :] + onehot[-1:, :]
    carry_ref[...] = total
    cnt_ref[...] = jnp.broadcast_to(total, cnt_ref.shape)

    route = jnp.zeros((tm, LANES), F32)
    fields = ((i1 - ROUTE_OFF).astype(F32), (i2 - ROUTE_OFF).astype(F32), gate1, gate2, rank1, rank2)
    for k, val in enumerate(fields):
        route = jnp.where(lane == k, val, route)
    route_ref[...] = route

    ple = _dot(p_ref[...].astype(BF16), wp_ref[...])
    gate = jax.nn.sigmoid(_dot(h.astype(BF16), wpg_ref[...]))
    base_ref[...] = alpha * h + ple * gate


def _post_call(sb, fx, x2, p2, wo, g1, b1, wr, br, wpg, wp, alpha):
    n, d = x2.shape
    tm = min(POST_TILE, n)
    row = lax.broadcasted_iota(jnp.int32, (tm, tm), 0)
    col = lax.broadcasted_iota(jnp.int32, (tm, tm), 1)
    tri = (col < row).astype(BF16)
    const = lambda i: (0, 0)
    tile = lambda i: (i, 0)
    return pl.pallas_call(
        functools.partial(_post_kernel, alpha=alpha),
        grid=(n // tm,),
        in_specs=[
            pl.BlockSpec((tm, SB_WIDTH), tile),
            pl.BlockSpec((tm, FOX_WIDTH), tile),
            pl.BlockSpec((tm, d), tile),
            pl.BlockSpec((tm, p2.shape[1]), tile),
            pl.BlockSpec(wo.shape, const),
            pl.BlockSpec((1, d), const),
            pl.BlockSpec((1, d), const),
            pl.BlockSpec(wr.shape, const),
            pl.BlockSpec((1, LANES), const),
            pl.BlockSpec(wpg.shape, const),
            pl.BlockSpec(wp.shape, const),
            pl.BlockSpec((tm, tm), const),
        ],
        out_specs=[
            pl.BlockSpec((tm, d), tile),
            pl.BlockSpec((tm, d), tile),
            pl.BlockSpec((tm, LANES), tile),
            pl.BlockSpec((8, LANES), const),
        ],
        out_shape=[
            jax.ShapeDtypeStruct((n, d), F32),
            jax.ShapeDtypeStruct((n, d), F32),
            jax.ShapeDtypeStruct((n, LANES), F32),
            jax.ShapeDtypeStruct((8, LANES), F32),
        ],
        scratch_shapes=[pltpu.VMEM((1, LANES), F32)],
        compiler_params=_cparams("arbitrary"),
        name="post_attn",
    )(sb, fx, x2, p2, wo, g1, b1, wr, br, wpg, wp, tri)


RANK_BITS = 20
ISSUE_UNROLL = 8


def _slot_dest(code_ref, start_ref, idx):
    code = code_ref[idx]
    return start_ref[lax.shift_right_logical(code, RANK_BITS)] + (code & ((1 << RANK_BITS) - 1))


def _dispatch_kernel(code_ref, start_ref, h_ref, xs_in_ref, xs_ref, sem, *, tt):
    del xs_in_ref

    def copy(r, k):
        dest = _slot_dest(code_ref, start_ref, TOP_K * r + k)
        return pltpu.make_async_copy(h_ref.at[pl.ds(r, 1), :], xs_ref.at[pl.ds(dest, 1), :], sem)

    def start(g, c):
        for u in range(ISSUE_UNROLL):
            for k in range(TOP_K):
                copy(g * ISSUE_UNROLL + u, k).start()
        return c

    def wait(g, c):
        for u in range(ISSUE_UNROLL):
            for k in range(TOP_K):
                copy(g * ISSUE_UNROLL + u, k).wait()
        return c

    lax.fori_loop(0, tt // ISSUE_UNROLL, start, 0)
    lax.fori_loop(0, tt // ISSUE_UNROLL, wait, 0)


def _dispatch_call(h, code_flat, pad_start, xs_zero):
    n, d = h.shape
    tt = min(DISPATCH_TILE, n)
    return pl.pallas_call(
        functools.partial(_dispatch_kernel, tt=tt),
        grid=(n // tt,),
        in_specs=[
            pl.BlockSpec((TOP_K * tt,), lambda i: (i,), memory_space=pltpu.SMEM),
            pl.BlockSpec(memory_space=pltpu.SMEM),
            pl.BlockSpec((tt, d), lambda i: (i, 0)),
            pl.BlockSpec(memory_space=pl.ANY),
        ],
        out_specs=pl.BlockSpec(memory_space=pl.ANY),
        out_shape=jax.ShapeDtypeStruct(xs_zero.shape, xs_zero.dtype),
        scratch_shapes=[pltpu.SemaphoreType.DMA(())],
        input_output_aliases={3: 0},
        compiler_params=_cparams("arbitrary"),
        name="moe_dispatch",
    )(code_flat, pad_start, h, xs_zero)


def _expert_kernel(be_ref, nb_ref, xs_ref, wg_ref, wu_ref, wd_ref, ys_ref, wg_bf, wu_bf, wd_bf):
    i = pl.program_id(0)
    used = i < nb_ref[0]

    @pl.when(used & ((i == 0) | (be_ref[i] != be_ref[jnp.maximum(i - 1, 0)])))
    def _():
        wg_bf[...] = wg_ref[0].astype(BF16)
        wu_bf[...] = wu_ref[0].astype(BF16)
        wd_bf[...] = wd_ref[0].astype(BF16)

    @pl.when(used)
    def _():
        xb = xs_ref[...].astype(BF16)
        g = _dot(xb, wg_bf[...])
        u = _dot(xb, wu_bf[...])
        hdn = (g * jax.nn.sigmoid(g)) * u
        ys_ref[...] = _dot(hdn.astype(BF16), wd_bf[...])

    @pl.when(jnp.logical_not(used))
    def _():
        ys_ref[...] = jnp.zeros_like(ys_ref)


def _expert_call(blk_expert, n_used, xs, wg, wu, wd):
    cap, d = xs.shape
    f = wg.shape[2]
    nblk = cap // MOE_BLOCK
    return pl.pallas_call(
        _expert_kernel,
        grid_spec=pltpu.PrefetchScalarGridSpec(
            num_scalar_prefetch=2,
            grid=(nblk,),
            in_specs=[
                pl.BlockSpec((MOE_BLOCK, d), lambda i, be, nb: (i, 0)),
                pl.BlockSpec((1, d, f), lambda i, be, nb: (be[i], 0, 0)),
                pl.BlockSpec((1, d, f), lambda i, be, nb: (be[i], 0, 0)),
                pl.BlockSpec((1, f, d), lambda i, be, nb: (be[i], 0, 0)),
            ],
            out_specs=pl.BlockSpec((MOE_BLOCK, d), lambda i, be, nb: (i, 0)),
            scratch_shapes=[pltpu.VMEM((d, f), BF16), pltpu.VMEM((d, f), BF16),
                            pltpu.VMEM((f, d), BF16)],
        ),
        out_shape=jax.ShapeDtypeStruct((cap, d), F32),
        compiler_params=_cparams("arbitrary"),
        name="moe_experts",
    )(blk_expert, n_used, xs, wg, wu, wd)


def _combine_kernel(code_ref, next_code_ref, start_ref, base_ref, route_ref, g2_ref, b2_ref, ys_ref,
                    o_ref, buf_ref, sems):
    tt = base_ref.shape[0]
    i = pl.program_id(0)
    last = pl.num_programs(0) - 1

    def copy(codes, buf, r, k):
        dest = _slot_dest(codes, start_ref, TOP_K * r + k)
        return pltpu.make_async_copy(ys_ref.at[pl.ds(dest, 1), :],
                                     buf_ref.at[buf, pl.ds(k * tt + r, 1), :], sems.at[buf])

    def for_all_rows(codes, buf, wait):
        def body(g, c):
            for u in range(ISSUE_UNROLL):
                for k in range(TOP_K):
                    cp = copy(codes, buf, g * ISSUE_UNROLL + u, k)
                    if wait:
                        cp.wait()
                    else:
                        cp.start(priority=k)
            return c
        lax.fori_loop(0, tt // ISSUE_UNROLL, body, 0)

    cur = i % 2

    @pl.when(i == 0)
    def _():
        for_all_rows(code_ref, 0, False)

    @pl.when(i < last)
    def _():
        for_all_rows(next_code_ref, 1 - cur, False)

    for_all_rows(code_ref, cur, True)

    route = route_ref[...]
    ffn = route[:, 2:3] * buf_ref[cur, 0:tt, :] + route[:, 3:4] * buf_ref[cur, tt:2 * tt, :]
    o_ref[...] = _layer_norm(base_ref[...] + ffn, g2_ref[...], b2_ref[...])


def _combine_call(code_flat, pad_start, base, route, g2, b2, ys):
    n, d = base.shape
    tt = min(MOVE_TILE, n)
    steps = n // tt
    return pl.pallas_call(
        _combine_kernel,
        grid=(steps,),
        in_specs=[
            pl.BlockSpec((TOP_K * tt,), lambda i: (i,), memory_space=pltpu.SMEM),
            pl.BlockSpec((TOP_K * tt,), lambda i: (jnp.minimum(i + 1, steps - 1),),
                         memory_space=pltpu.SMEM),
            pl.BlockSpec(memory_space=pltpu.SMEM),
            pl.BlockSpec((tt, d), lambda i: (i, 0)),
            pl.BlockSpec((tt, LANES), lambda i: (i, 0)),
            pl.BlockSpec((1, d), lambda i: (0, 0)),
            pl.BlockSpec((1, d), lambda i: (0, 0)),
            pl.BlockSpec(memory_space=pl.ANY),
        ],
        out_specs=pl.BlockSpec((tt, d), lambda i: (i, 0)),
        out_shape=jax.ShapeDtypeStruct((n, d), F32),
        scratch_shapes=[pltpu.VMEM((2, TOP_K * tt, d), F32), pltpu.SemaphoreType.DMA((2,))],
        compiler_params=_cparams("arbitrary"),
        name="moe_combine",
    )(code_flat, code_flat, pad_start, base, route, g2, b2, ys)


def _layer(h, p_i, w_in, b_forget, w_out, g1, b1, w_group, b_group, w_router, b_router,
           w_gate, w_up, w_down, w_ple, w_ple_gate, g2, b2, alpha):
    bsz, seq, d = h.shape
    n = bsz * seq
    x2 = h.reshape(n, d)

    wr = jnp.pad(jnp.concatenate([w_group, w_router], axis=1).astype(F32),
                 ((0, 0), (0, LANES - N_GROUPS - N_EXPERTS)))
    br = jnp.pad(jnp.concatenate([b_group, b_router]).astype(F32),
                 (0, LANES - N_GROUPS - N_EXPERTS)).reshape(1, LANES)

    qsb, qfx, kaug, vt, cfmin = _proj_call(x2, w_in, b_forget, seq)
    per_tile = min(ROW_TILE, seq) // ATT_BLOCK
    cfmin = cfmin[:, :per_tile, :FOX_HEADS].reshape(bsz, seq // ATT_BLOCK, FOX_HEADS)
    cfmin = cfmin.transpose(0, 2, 1).reshape(-1)

    sb = _sb_call(qsb.reshape(bsz, seq, SB_COLS)).reshape(n, SB_WIDTH)
    fx = _fox_call(qfx.reshape(bsz, seq, FOX_WIDTH), kaug.reshape(bsz, seq, FOX_HEADS * LANES),
                   vt, cfmin).reshape(n, FOX_WIDTH)

    base, h1, route, counts = _post_call(
        sb, fx, x2, p_i.reshape(n, -1), w_out.astype(BF16), g1.reshape(1, d), b1.reshape(1, d),
        wr, br, w_ple_gate.astype(BF16), w_ple.astype(BF16), alpha)

    counts = counts[0, ROUTE_OFF:ROUTE_OFF + N_EXPERTS].astype(jnp.int32)
    padded = ((counts + MOE_BLOCK - 1) // MOE_BLOCK) * MOE_BLOCK
    pad_end = jnp.cumsum(padded)
    pad_start = pad_end - padded
    eid = route[:, 0:TOP_K].astype(jnp.int32)
    rank = route[:, 4:4 + TOP_K].astype(jnp.int32)
    code = (eid * (1 << RANK_BITS) + rank).reshape(n * TOP_K)
    cap = n * TOP_K + N_EXPERTS * MOE_BLOCK
    nblk = cap // MOE_BLOCK
    blk_first = jnp.arange(nblk, dtype=jnp.int32)[:, None] * MOE_BLOCK
    blk_expert = jnp.minimum(jnp.sum((pad_end[None, :] <= blk_first).astype(jnp.int32), axis=1),
                             N_EXPERTS - 1)
    n_used = (pad_end[-1:] // MOE_BLOCK).astype(jnp.int32)

    xs = _dispatch_call(h1, code, pad_start, jnp.zeros((cap, d), F32))
    ys = _expert_call(blk_expert, n_used, xs, w_gate, w_up, w_down)
    out = _combine_call(code, pad_start, base, route, g2.reshape(1, d), b2.reshape(1, d), ys)
    return out.reshape(bsz, seq, d)


def kernel(x, p, w_in, b_forget, w_out, ln_mix_g, ln_mix_b, w_group, b_group, w_router, b_router,
           w_gate, w_up, w_down, w_ple, w_ple_gate, ln_ffn_g, ln_ffn_b):
    depth = w_in.shape[0]
    alpha = (2 * depth) ** 0.25
    h = x
    for i in range(depth):
        h = _layer(h, p[i], w_in[i], b_forget[i], w_out[i], ln_mix_g[i], ln_mix_b[i], w_group[i],
                   b_group[i], w_router[i], b_router[i], w_gate[i], w_up[i], w_down[i], w_ple[i],
                   w_ple_gate[i], ln_ffn_g[i], ln_ffn_b[i], alpha)
    return h
```

```python
import functools

import jax
import jax.numpy as jnp
from jax import lax
from jax.experimental import pallas as pl
from jax.experimental.pallas import tpu as pltpu

F32 = jnp.float32
BF16 = jnp.bfloat16

HEAD_DIM = 64
SB_HEADS = 8
FOX_HEADS = 8
SB_WIDTH = SB_HEADS * HEAD_DIM
FOX_WIDTH = FOX_HEADS * HEAD_DIM
N_GROUPS = 4
EXPERTS_PER_GROUP = 8
N_EXPERTS = N_GROUPS * EXPERTS_PER_GROUP
TOP_K = 2
LN_EPS = 1e-5
SCALE = HEAD_DIM ** -0.5

LANES = 128
ATT_BLOCK = 256
ROW_TILE = 512
POST_TILE = 512
MOE_BLOCK = 256
MOVE_TILE = 256
DISPATCH_TILE = 512
ROUTE_OFF = N_GROUPS
VMEM_LIMIT = 56 * 1024 * 1024


def _cparams(*sem):
    return pltpu.CompilerParams(dimension_semantics=sem, vmem_limit_bytes=VMEM_LIMIT)


def _dot(a, b):
    return jnp.dot(a, b, preferred_element_type=F32)


def _dot_nt(a, b):
    return lax.dot_general(a, b, (((1,), (1,)), ((), ())), preferred_element_type=F32)


def _split2(x):
    hi = x.astype(BF16)
    lo = (x - hi.astype(F32)).astype(BF16)
    return hi, lo


def _split3(x):
    hi = x.astype(BF16)
    r = x - hi.astype(F32)
    mid = r.astype(BF16)
    lo = (r - mid.astype(F32)).astype(BF16)
    return hi, mid, lo


def _layer_norm(u, g, b):
    mu = jnp.mean(u, axis=-1, keepdims=True)
    d = u - mu
    var = jnp.mean(d * d, axis=-1, keepdims=True)
    return d * lax.rsqrt(var + LN_EPS) * g + b


SB_COLS = 3 * SB_WIDTH
FXK_OFF = SB_COLS + FOX_WIDTH
FGATE_OFF = FXK_OFF + FOX_HEADS * LANES
PROJ_COLS = FGATE_OFF + LANES
AUG_LANE = HEAD_DIM
AUG_TERMS = 3


def _proj_kernel(x_ref, w_ref, wvt_ref, bf_ref, tri_ref, sel_ref, qsb_ref, qfx_ref, kaug_ref,
                 vt_ref, cfmin_ref, carry_ref, *, tiles_per_seq):
    i = pl.program_id(0)
    t = ATT_BLOCK

    @pl.when(i % tiles_per_seq == 0)
    def _():
        carry_ref[...] = jnp.zeros_like(carry_ref)

    xb = x_ref[...].astype(BF16)
    tm = xb.shape[0]
    chunk = 512
    for c in range(SB_COLS // chunk):
        sl = slice(c * chunk, (c + 1) * chunk)
        qsb_ref[:, sl] = _dot(xb, w_ref[:, sl]).astype(BF16)
    qfx_ref[...] = _dot(xb, w_ref[:, SB_COLS:FXK_OFF]).astype(BF16)

    fl = _dot(xb, w_ref[:, FGATE_OFF:]) + bf_ref[...]
    lf = jnp.minimum(fl, 0.0) - jnp.log(1.0 + jnp.exp(-jnp.abs(fl)))
    tri = tri_ref[...]
    hi, mid, lo = _split3(lf)
    cum = _dot(tri, hi) + _dot(tri, mid) + _dot(tri, lo) + carry_ref[...]
    carry_ref[...] = cum[-1:, :]

    cum2 = cum * LOG2E
    nhi, nmid, nlo = _split3(-cum2)
    lane = lax.broadcasted_iota(jnp.int32, (tm, LANES), 1)
    terms = jnp.where(lane < FOX_HEADS, nhi, jnp.where(lane < 2 * FOX_HEADS, nmid, nlo))
    aug = _dot(terms, sel_ref[...])
    for c in range((FGATE_OFF - FXK_OFF) // chunk):
        sl = slice(c * chunk, (c + 1) * chunk)
        wsl = slice(FXK_OFF + c * chunk, FXK_OFF + (c + 1) * chunk)
        kaug_ref[:, sl] = (_dot(xb, w_ref[:, wsl]) * LOG2E + aug[:, sl]).astype(BF16)

    mins = []
    for c in range(tm // t):
        rows = slice(c * t, (c + 1) * t)
        vt_ref[c] = _dot_nt(wvt_ref[...], xb[rows, :]).astype(BF16)
        mins.append(jnp.min(cum2[rows, :], axis=0, keepdims=True))
    mins.append(jnp.zeros((8 - tm // t, LANES), F32))
    cfmin_ref[0] = jnp.concatenate(mins, axis=0)


def _proj_weights(w_in, b_forget):
    d = w_in.shape[0]
    off_q_fx = SB_COLS
    off_k_fx = off_q_fx + FOX_WIDTH
    off_v_fx = off_k_fx + FOX_WIDTH
    off_f = off_v_fx + FOX_WIDTH
    wk = w_in[:, off_k_fx:off_v_fx].reshape(d, FOX_HEADS, HEAD_DIM)
    wk = jnp.pad(wk, ((0, 0), (0, 0), (0, LANES - HEAD_DIM))).reshape(d, FOX_HEADS * LANES)
    wf = jnp.tile(w_in[:, off_f:off_f + FOX_HEADS], (1, AUG_TERMS))
    wf = jnp.pad(wf, ((0, 0), (0, LANES - AUG_TERMS * FOX_HEADS)))
    w_pad = jnp.concatenate([w_in[:, :off_k_fx], wk, wf], axis=1).astype(BF16)
    wvt = w_in[:, off_v_fx:off_f].T.astype(BF16)
    bf = jnp.pad(jnp.tile(b_forget.astype(F32), AUG_TERMS), (0, LANES - AUG_TERMS * FOX_HEADS))
    src = lax.broadcasted_iota(jnp.int32, (LANES, FOX_HEADS * LANES), 0)
    dst = lax.broadcasted_iota(jnp.int32, (LANES, FOX_HEADS * LANES), 1)
    head, term = src % FOX_HEADS, src // FOX_HEADS
    sel = ((term < AUG_TERMS) & (dst == head * LANES + AUG_LANE + term)).astype(BF16)
    return w_pad, wvt, bf.reshape(1, LANES), sel


def _proj_call(x2, w_in, b_forget, seq):
    n, d = x2.shape
    tm = min(ROW_TILE, seq)
    t = ATT_BLOCK
    w_pad, wvt, bf_pad, sel = _proj_weights(w_in, b_forget)
    row = lax.broadcasted_iota(jnp.int32, (tm, tm), 0)
    col = lax.broadcasted_iota(jnp.int32, (tm, tm), 1)
    tri = (col <= row).astype(BF16)
    const = lambda i: (0, 0)
    tile = lambda i: (i, 0)
    return pl.pallas_call(
        functools.partial(_proj_kernel, tiles_per_seq=seq // tm),
        grid=(n // tm,),
        in_specs=[
            pl.BlockSpec((tm, d), tile),
            pl.BlockSpec((d, PROJ_COLS), const),
            pl.BlockSpec((FOX_WIDTH, d), const),
            pl.BlockSpec((1, LANES), const),
            pl.BlockSpec((tm, tm), const),
            pl.BlockSpec(sel.shape, const),
        ],
        out_specs=[
            pl.BlockSpec((tm, SB_COLS), tile),
            pl.BlockSpec((tm, FOX_WIDTH), tile),
            pl.BlockSpec((tm, FOX_HEADS * LANES), tile),
            pl.BlockSpec((tm // t, FOX_WIDTH, t), lambda i: (i, 0, 0)),
            pl.BlockSpec((1, 8, LANES), lambda i: (i, 0, 0)),
        ],
        out_shape=[
            jax.ShapeDtypeStruct((n, SB_COLS), BF16),
            jax.ShapeDtypeStruct((n, FOX_WIDTH), BF16),
            jax.ShapeDtypeStruct((n, FOX_HEADS * LANES), BF16),
            jax.ShapeDtypeStruct((n // t, FOX_WIDTH, t), BF16),
            jax.ShapeDtypeStruct((n // tm, 8, LANES), F32),
        ],
        scratch_shapes=[pltpu.VMEM((1, LANES), F32)],
        compiler_params=_cparams("arbitrary"),
        name="proj",
    )(x2, w_pad, wvt, bf_pad, tri, sel)


ZERO_EXP = 110.0
ZERO_EXP2 = 160.0
LOG2E = 1.4426950408889634


def _softplus(z):
    return jnp.maximum(z, 0.0) + jnp.log(1.0 + jnp.exp(-jnp.abs(z)))


def _key_absmax(k_ref, kam_ref):
    seq, width = k_ref.shape[1], k_ref.shape[2]
    rows = min(512, seq)

    def body(i, m):
        blk = k_ref[0, pl.ds(pl.multiple_of(i * rows, rows), rows), :]
        return jnp.maximum(m, jnp.max(jnp.abs(blk.astype(F32)), axis=0, keepdims=True))

    kam = lax.fori_loop(0, seq // rows, body, jnp.zeros((1, width), F32))
    kam_ref[...] = jnp.broadcast_to(kam, kam_ref.shape)


def _score_bound(qh, kam):
    zmax = jnp.sum(jnp.abs(qh.astype(F32)) * kam, axis=-1, keepdims=True)
    return zmax * (1.0 + 2.0 ** -10) + 2.0 ** -10


SB_QBLOCKS = 2


def _sb_kernel(q_ref, k_ref, v_ref, tri_ref, o_ref, kam_ref):
    t = ATT_BLOCK
    step = pl.program_id(2)

    @pl.when(step == 0)
    def _():
        _key_absmax(k_ref, kam_ref)

    lane = lax.broadcasted_iota(jnp.int32, (t, LANES), 1)
    row = lax.broadcasted_iota(jnp.int32, (t, t), 0)
    col = lax.broadcasted_iota(jnp.int32, (t, t), 1)
    earlier = col < row
    tri = tri_ref[...]
    kam = kam_ref[0:1, :]

    def block(qs, kb, state, diag, valid=None):
        start = pl.multiple_of(kb * t, t)
        kblk = k_ref[0, pl.ds(start, t), :]
        vblk = v_ref[0, pl.ds(start, t), :]
        new = []
        for h in range(2):
            carry, acc = state[2 * h], state[2 * h + 1]
            z = _dot_nt(qs[h], kblk)
            sp = _softplus(z)
            if diag:
                sp = jnp.where(earlier, sp, 0.0)
            if valid is not None:
                sp = jnp.where(valid, sp, 0.0)
            hi, lo = _split2(sp)
            cum = _dot(hi, tri) + _dot(lo, tri)
            w = jnp.exp(z - cum - carry)
            if diag:
                w = jnp.where(earlier, w, 0.0)
            if valid is not None:
                w = jnp.where(valid, w, 0.0)
            acc = acc + _dot(w.astype(BF16), vblk)
            carry = carry + cum[:, 0:1]
            new += [carry, acc]
        return tuple(new)

    init = (jnp.zeros((t, 1), F32), jnp.zeros((t, LANES), F32)) * 2
    streams = []
    for x in range(SB_QBLOCKS):
        qi = step * SB_QBLOCKS + x
        q = q_ref[0, x * t:(x + 1) * t, :] * jnp.asarray(SCALE, BF16)
        zero_q = jnp.zeros_like(q)
        qs = (jnp.where(lane < HEAD_DIM, q, zero_q), jnp.where(lane >= HEAD_DIM, q, zero_q))
        zmax = tuple(_score_bound(qh, kam) for qh in qs)
        state = block(qs, qi, init, True)
        state = block(qs, jnp.maximum(qi - 1, 0), state, False, valid=(qi >= 1) if x == 0 else None)
        streams.append((qi, qs, zmax, state))

    for x, (qi, qs, zmax, state) in enumerate(streams):
        def live(state, zmax=zmax):
            worst = jnp.maximum(jnp.max(zmax[0] - state[0]), jnp.max(zmax[1] - state[2]))
            return worst > -ZERO_EXP

        def cond(c):
            return (c[0] >= 0) & c[1]

        def body(c, qs=qs, live=live):
            state = block(qs, c[0], c[2:], False)
            return (c[0] - 1, live(state)) + state

        c = lax.while_loop(cond, body, (qi - 2, live(state)) + state)
        o_ref[0, x * t:(x + 1) * t, :] = jnp.where(lane < HEAD_DIM, c[3], c[5]).astype(o_ref.dtype)


def _att_tri():
    t = ATT_BLOCK
    s = lax.broadcasted_iota(jnp.int32, (t, t), 0)
    j = lax.broadcasted_iota(jnp.int32, (t, t), 1)
    return (s >= j).astype(BF16)


def _sb_call(qkv):
    b, seq, _ = qkv.shape
    t = ATT_BLOCK
    pairs = SB_WIDTH // LANES
    q_off, k_off, v_off = 0, SB_WIDTH // LANES, 2 * SB_WIDTH // LANES
    rows = SB_QBLOCKS * t
    return pl.pallas_call(
        _sb_kernel,
        grid=(b, pairs, seq // rows),
        in_specs=[
            pl.BlockSpec((1, rows, LANES), lambda bi, hp, qi: (bi, qi, q_off + hp)),
            pl.BlockSpec((1, seq, LANES), lambda bi, hp, qi: (bi, 0, k_off + hp)),
            pl.BlockSpec((1, seq, LANES), lambda bi, hp, qi: (bi, 0, v_off + hp)),
            pl.BlockSpec((t, t), lambda bi, hp, qi: (0, 0)),
        ],
        out_specs=pl.BlockSpec((1, rows, LANES), lambda bi, hp, qi: (bi, qi, hp)),
        out_shape=jax.ShapeDtypeStruct((b, seq, SB_WIDTH), BF16),
        scratch_shapes=[pltpu.VMEM((8, LANES), F32)],
        compiler_params=_cparams("arbitrary", "arbitrary", "arbitrary"),
        name="sb_attn",
    )(qkv, qkv, qkv, _att_tri())


FOX_QBLOCKS = 2


def _fox_kernel(cfmin_ref, q_ref, k_ref, vt_ref, o_ref, s_a, s_b, p_a, p_b, kam_ref, *, nblk):
    t = ATT_BLOCK
    bi, hp, step = pl.program_id(0), pl.program_id(1), pl.program_id(2)

    @pl.when(step == 0)
    def _():
        _key_absmax(k_ref, kam_ref)

    lane = lax.broadcasted_iota(jnp.int32, (t, LANES), 1)
    feat = lax.broadcasted_iota(jnp.int32, (LANES, t), 0)
    key = lax.broadcasted_iota(jnp.int32, (t, t), 0)
    qry = lax.broadcasted_iota(jnp.int32, (t, t), 1)
    visible = key <= qry
    ones_aug = jnp.where(lane < AUG_LANE + AUG_TERMS, 1.0, 0.0)
    own_rows = (feat < HEAD_DIM, feat >= HEAD_DIM)
    neg_inf = jnp.full((t, t), -jnp.inf, F32)

    chains = [(x, h) for x in range(FOX_QBLOCKS) for h in range(2)]
    qis = [step * FOX_QBLOCKS + x for x in range(FOX_QBLOCKS)]
    qt, zmax = [], []
    for x, h in chains:
        qf = q_ref[0, x * t:(x + 1) * t, :].astype(F32) * SCALE
        qh = qf if h == 0 else pltpu.roll(qf, HEAD_DIM, 1)
        qht = jnp.where(lane < HEAD_DIM, qh, ones_aug).T
        qt.append(qht.astype(BF16))
        kam = kam_ref[0:1, h * LANES:(h + 1) * LANES] * (1.0 + 2.0 ** -7)
        kam16 = jnp.broadcast_to(jnp.where(lane[0:1, :] < HEAD_DIM, kam, 0.0), (16, LANES)).astype(BF16)
        aq = jnp.where(feat < HEAD_DIM, jnp.abs(qht), 0.0).astype(BF16)
        zmax.append(_dot(kam16, aq)[0:1, :] * (1.0 + 2.0 ** -10) + 2.0 ** -10)

    def qk(c, off):
        h = chains[c][1]
        kb = jnp.maximum(qis[chains[c][0]] - off, 0)
        start = pl.multiple_of(kb * t, t)
        return _dot(k_ref[0, pl.ds(start, t), h * LANES:(h + 1) * LANES], qt[c])

    def softmax_part(s, m):
        m_new = jnp.maximum(m, jnp.max(s, axis=0, keepdims=True))
        return m_new, jnp.exp2(s - m_new).astype(BF16), jnp.exp2(m - m_new)

    def pv(c, off, alpha, acc, p_buf):
        vt = vt_ref[jnp.maximum(qis[chains[c][0]] - off, 0)]
        vmod = jnp.where(own_rows[chains[c][1]], vt, jnp.ones_like(vt))
        return alpha * acc + _dot(vmod, p_buf[c])

    def live(off, ms):
        alive = None
        for c, (x, h) in enumerate(chains):
            kb = qis[x] - off
            cf_min = cfmin_ref[(bi * FOX_HEADS + 2 * hp + h) * nblk + jnp.maximum(kb, 0)]
            a_c = (kb >= 0) & (jnp.max(zmax[c] - ms[c]) - cf_min > -ZERO_EXP2)
            alive = a_c if alive is None else alive | a_c
        return alive

    def scores(c, off, raw):
        return jnp.where(qis[chains[c][0]] - off >= 0, raw, neg_inf)

    n = len(chains)
    s_bufs, p_bufs = (s_a, s_b), (p_a, p_b)
    ms, alphas = [], []
    for c in range(n):
        m, p, alpha = softmax_part(jnp.where(visible, qk(c, 0), neg_inf), jnp.full((1, t), -jnp.inf, F32))
        p_a[c] = p
        s_b[c] = qk(c, 1)
        ms.append(m)
        alphas.append(alpha)
    accs = [jnp.zeros((LANES, t), F32)] * n

    def cond(st):
        return (st[0] <= qis[-1]) & st[1]

    def body(st):
        j = st[0]
        ms, alphas, accs = list(st[2:2 + n]), list(st[2 + n:2 + 2 * n]), list(st[2 + 2 * n:])
        for half in range(2):
            off = j + half
            p_old, p_new = p_bufs[half], p_bufs[1 - half]
            s_cur, s_nxt = s_bufs[1 - half], s_bufs[half]
            for c in range(n):
                accs[c] = pv(c, off - 1, alphas[c], accs[c], p_old)
            for c in range(n):
                ms[c], p, alphas[c] = softmax_part(scores(c, off, s_cur[c]), ms[c])
                p_new[c] = p
            for c in range(n):
                s_nxt[c] = qk(c, off + 1)
        return (j + 2, live(j + 2, ms)) + tuple(ms) + tuple(alphas) + tuple(accs)

    st = lax.while_loop(cond, body, (jnp.int32(1), live(1, ms)) + tuple(ms) + tuple(alphas) + tuple(accs))
    j = st[0]
    accs = [pv(c, j - 1, st[2 + n + c], st[2 + 2 * n + c], p_a) for c in range(n)]
    for x in range(FOX_QBLOCKS):
        acc0, acc1 = accs[2 * x], accs[2 * x + 1]
        out_t = jnp.concatenate([acc0[:HEAD_DIM] / acc0[HEAD_DIM:HEAD_DIM + 1],
                                 acc1[HEAD_DIM:] / acc1[0:1]], axis=0)
        o_ref[0, x * t:(x + 1) * t, :] = out_t.T.astype(o_ref.dtype)


def _fox_call(qfx, kaug, vt, cfmin):
    b, seq, _ = qfx.shape
    t = ATT_BLOCK
    nblk = seq // t
    rows = FOX_QBLOCKS * t
    chains = 2 * FOX_QBLOCKS
    pairs = FOX_WIDTH // LANES
    return pl.pallas_call(
        functools.partial(_fox_kernel, nblk=nblk),
        grid=(b, pairs, seq // rows),
        in_specs=[
            pl.BlockSpec(memory_space=pltpu.SMEM),
            pl.BlockSpec((1, rows, LANES), lambda bi, hp, qi: (bi, qi, hp)),
            pl.BlockSpec((1, seq, 2 * LANES), lambda bi, hp, qi: (bi, 0, hp)),
            pl.BlockSpec((nblk, LANES, t), lambda bi, hp, qi: (bi, hp, 0)),
        ],
        out_specs=pl.BlockSpec((1, rows, LANES), lambda bi, hp, qi: (bi, qi, hp)),
        out_shape=jax.ShapeDtypeStruct((b, seq, FOX_WIDTH), BF16),
        scratch_shapes=[pltpu.VMEM((chains, t, t), F32), pltpu.VMEM((chains, t, t), F32),
                        pltpu.VMEM((chains, t, t), BF16), pltpu.VMEM((chains, t, t), BF16),
                        pltpu.VMEM((8, 2 * LANES), F32)],
        compiler_params=_cparams("arbitrary", "arbitrary", "arbitrary"),
        name="fox_attn",
    )(cfmin, qfx, kaug, vt)


def _post_kernel(sb_ref, fx_ref, x_ref, p_ref, wo_ref, g1_ref, b1_ref, wr_ref, br_ref, wpg_ref,
                 wp_ref, tri_ref, base_ref, h_ref, route_ref, cnt_ref, carry_ref, *, alpha):
    i = pl.program_id(0)

    @pl.when(i == 0)
    def _():
        carry_ref[...] = jnp.zeros_like(carry_ref)

    tm = POST_TILE // 2
    for half in range(x_ref.shape[0] // tm):
        rows = slice(half * tm, (half + 1) * tm)
        _post_rows(sb_ref[rows, :], fx_ref[rows, :], x_ref[rows, :], p_ref[rows, :], wo_ref, g1_ref,
                   b1_ref, wr_ref, br_ref, wpg_ref, wp_ref, tri_ref, base_ref.at[rows, :],
                   h_ref.at[rows, :], route_ref.at[rows, :], cnt_ref, carry_ref, alpha)


def _post_rows(sb, fx, x, p, wo_ref, g1_ref, b1_ref, wr_ref, br_ref, wpg_ref, wp_ref, tri_ref,
               base_ref, h_ref, route_ref, cnt_ref, carry_ref, alpha):
    mix = _dot(sb, wo_ref[:SB_WIDTH, :]) + _dot(fx, wo_ref[SB_WIDTH:, :])
    h = _layer_norm(alpha * x + mix, g1_ref[...], b1_ref[...])
    h_ref[...] = h

    tm = h.shape[0]
    lane = lax.broadcasted_iota(jnp.int32, (tm, LANES), 1)
    big = jnp.int32(4 * LANES)
    h_hi, h_lo = _split2(h)
    hw = _dot(h_hi, wr_ref[...])
    logits = hw[:, :LANES] + hw[:, LANES:] + _dot(h_lo, wr_ref[:, :LANES]) + br_ref[...]

    def first_argmax(vals):
        vmax = jnp.max(vals, axis=-1, keepdims=True)
        idx = jnp.min(jnp.where(vals == vmax, lane, big), axis=-1, keepdims=True)
        return vmax, idx

    neg = -jnp.inf
    gl = jnp.where(lane < N_GROUPS, logits, neg)
    gmax, gsel = first_argmax(gl)
    g_prob = 1.0 / jnp.sum(jnp.exp(gl - gmax), axis=-1, keepdims=True)
    lo = ROUTE_OFF + gsel * EXPERTS_PER_GROUP
    el = jnp.where((lane >= lo) & (lane < lo + EXPERTS_PER_GROUP), logits, neg)
    v1, i1 = first_argmax(el)
    v2, i2 = first_argmax(jnp.where(lane == i1, neg, el))
    e2 = jnp.exp(v2 - v1)
    gate1 = g_prob / (1.0 + e2)
    gate2 = g_prob * e2 / (1.0 + e2)

    hit1 = lane == i1
    hit2 = lane == i2
    onehot = jnp.where(hit1 | hit2, 1.0, 0.0)
    before = _dot(tri_ref[...], onehot.astype(BF16)) + carry_ref[...]
    rank1 = jnp.sum(jnp.where(hit1, before, 0.0), axis=-1, keepdims=True)
    rank2 = jnp.sum(jnp.where(hit2, before, 0.0), axis=-1, keepdims=True)
    total = before[-1:, :] + onehot[-1:, :]
    carry_ref[...] = total
    cnt_ref[...] = jnp.broadcast_to(total, cnt_ref.shape)

    route = jnp.zeros((tm, LANES), F32)
    fields = ((i1 - ROUTE_OFF).astype(F32), (i2 - ROUTE_OFF).astype(F32), gate1, gate2, rank1, rank2)
    for k, val in enumerate(fields):
        route = jnp.where(lane == k, val, route)
    route_ref[...] = route

    ple = _dot(p.astype(BF16), wp_ref[...])
    gate = jax.nn.sigmoid(_dot(h_hi, wpg_ref[...]))
    base_ref[...] = alpha * h + ple * gate


def _post_call(sb, fx, x2, p2, wo, g1, b1, wr, br, wpg, wp, alpha):
    n, d = x2.shape
    tm = POST_TILE
    half = POST_TILE // 2
    row = lax.broadcasted_iota(jnp.int32, (half, half), 0)
    col = lax.broadcasted_iota(jnp.int32, (half, half), 1)
    tri = (col < row).astype(BF16)
    const = lambda i: (0, 0)
    tile = lambda i: (i, 0)
    return pl.pallas_call(
        functools.partial(_post_kernel, alpha=alpha),
        grid=(n // tm,),
        in_specs=[
            pl.BlockSpec((tm, SB_WIDTH), tile),
            pl.BlockSpec((tm, FOX_WIDTH), tile),
            pl.BlockSpec((tm, d), tile),
            pl.BlockSpec((tm, p2.shape[1]), tile),
            pl.BlockSpec(wo.shape, const),
            pl.BlockSpec((1, d), const),
            pl.BlockSpec((1, d), const),
            pl.BlockSpec(wr.shape, const),
            pl.BlockSpec((1, LANES), const),
            pl.BlockSpec(wpg.shape, const),
            pl.BlockSpec(wp.shape, const),
            pl.BlockSpec(tri.shape, const),
        ],
        out_specs=[
            pl.BlockSpec((tm, d), tile),
            pl.BlockSpec((tm, d), tile),
            pl.BlockSpec((tm, LANES), tile),
            pl.BlockSpec((8, LANES), const),
        ],
        out_shape=[
            jax.ShapeDtypeStruct((n, d), F32),
            jax.ShapeDtypeStruct((n, d), F32),
            jax.ShapeDtypeStruct((n, LANES), F32),
            jax.ShapeDtypeStruct((8, LANES), F32),
        ],
        scratch_shapes=[pltpu.VMEM((1, LANES), F32)],
        compiler_params=_cparams("arbitrary"),
        name="post_attn",
    )(sb, fx, x2, p2, wo, g1, b1, wr, br, wpg, wp, tri)


RANK_BITS = 20
ISSUE_UNROLL = 8


def _slot_dest(code_ref, start_ref, idx):
    code = code_ref[idx]
    return start_ref[lax.shift_right_logical(code, RANK_BITS)] + (code & ((1 << RANK_BITS) - 1))


def _dispatch_kernel(code_ref, start_ref, h_ref, xs_in_ref, xs_ref, sem, *, tt):
    del xs_in_ref

    def copy(r, k):
        dest = _slot_dest(code_ref, start_ref, TOP_K * r + k)
        return pltpu.make_async_copy(h_ref.at[pl.ds(r, 1), :], xs_ref.at[pl.ds(dest, 1), :], sem)

    def start(g, c):
        for u in range(ISSUE_UNROLL):
            for k in range(TOP_K):
                copy(g * ISSUE_UNROLL + u, k).start()
        return c

    def wait(g, c):
        for u in range(ISSUE_UNROLL):
            for k in range(TOP_K):
                copy(g * ISSUE_UNROLL + u, k).wait()
        return c

    lax.fori_loop(0, tt // ISSUE_UNROLL, start, 0)
    lax.fori_loop(0, tt // ISSUE_UNROLL, wait, 0)


def _dispatch_call(h, code_flat, pad_start, xs_zero):
    n, d = h.shape
    tt = min(DISPATCH_TILE, n)
    return pl.pallas_call(
        functools.partial(_dispatch_kernel, tt=tt),
        grid=(n // tt,),
        in_specs=[
            pl.BlockSpec((TOP_K * tt,), lambda i: (i,), memory_space=pltpu.SMEM),
            pl.BlockSpec(memory_space=pltpu.SMEM),
            pl.BlockSpec((tt, d), lambda i: (i, 0)),
            pl.BlockSpec(memory_space=pl.ANY),
        ],
        out_specs=pl.BlockSpec(memory_space=pl.ANY),
        out_shape=jax.ShapeDtypeStruct(xs_zero.shape, xs_zero.dtype),
        scratch_shapes=[pltpu.SemaphoreType.DMA(())],
        input_output_aliases={3: 0},
        compiler_params=_cparams("arbitrary"),
        name="moe_dispatch",
    )(code_flat, pad_start, h, xs_zero)


def _expert_kernel(be_ref, nb_ref, xs_ref, wg_ref, wu_ref, wd_ref, ys_ref, wg_bf, wu_bf, wd_bf):
    i = pl.program_id(0)
    used = i < nb_ref[0]

    @pl.when(used & ((i == 0) | (be_ref[i] != be_ref[jnp.maximum(i - 1, 0)])))
    def _():
        wg_bf[...] = wg_ref[0].astype(BF16)
        wu_bf[...] = wu_ref[0].astype(BF16)
        wd_bf[...] = wd_ref[0].astype(BF16)

    @pl.when(used)
    def _():
        xb = xs_ref[...].astype(BF16)
        g = _dot(xb, wg_bf[...])
        u = _dot(xb, wu_bf[...])
        hdn = (g * jax.nn.sigmoid(g)) * u
        ys_ref[...] = _dot(hdn.astype(BF16), wd_bf[...])

    @pl.when(jnp.logical_not(used))
    def _():
        ys_ref[...] = jnp.zeros_like(ys_ref)


def _expert_call(blk_expert, n_used, xs, wg, wu, wd):
    cap, d = xs.shape
    f = wg.shape[2]
    nblk = cap // MOE_BLOCK
    return pl.pallas_call(
        _expert_kernel,
        grid_spec=pltpu.PrefetchScalarGridSpec(
            num_scalar_prefetch=2,
            grid=(nblk,),
            in_specs=[
                pl.BlockSpec((MOE_BLOCK, d), lambda i, be, nb: (i, 0)),
                pl.BlockSpec((1, d, f), lambda i, be, nb: (be[i], 0, 0)),
                pl.BlockSpec((1, d, f), lambda i, be, nb: (be[i], 0, 0)),
                pl.BlockSpec((1, f, d), lambda i, be, nb: (be[i], 0, 0)),
            ],
            out_specs=pl.BlockSpec((MOE_BLOCK, d), lambda i, be, nb: (i, 0)),
            scratch_shapes=[pltpu.VMEM((d, f), BF16), pltpu.VMEM((d, f), BF16),
                            pltpu.VMEM((f, d), BF16)],
        ),
        out_shape=jax.ShapeDtypeStruct((cap, d), F32),
        compiler_params=_cparams("arbitrary"),
        name="moe_experts",
    )(blk_expert, n_used, xs, wg, wu, wd)


def _combine_kernel(code_ref, next_code_ref, start_ref, base_ref, route_ref, g2_ref, b2_ref, ys_ref,
                    o_ref, buf_ref, sems):
    tt = base_ref.shape[0]
    i = pl.program_id(0)
    last = pl.num_programs(0) - 1

    def copy(codes, buf, r, k):
        dest = _slot_dest(codes, start_ref, TOP_K * r + k)
        return pltpu.make_async_copy(ys_ref.at[pl.ds(dest, 1), :],
                                     buf_ref.at[buf, pl.ds(k * tt + r, 1), :], sems.at[buf])

    def for_all_rows(codes, buf, wait):
        def body(g, c):
            for u in range(ISSUE_UNROLL):
                for k in range(TOP_K):
                    cp = copy(codes, buf, g * ISSUE_UNROLL + u, k)
                    if wait:
                        cp.wait()
                    else:
                        cp.start(priority=k)
            return c
        lax.fori_loop(0, tt // ISSUE_UNROLL, body, 0)

    cur = i % 2

    @pl.when(i == 0)
    def _():
        for_all_rows(code_ref, 0, False)

    @pl.when(i < last)
    def _():
        for_all_rows(next_code_ref, 1 - cur, False)

    for_all_rows(code_ref, cur, True)

    route = route_ref[...]
    ffn = route[:, 2:3] * buf_ref[cur, 0:tt, :] + route[:, 3:4] * buf_ref[cur, tt:2 * tt, :]
    o_ref[...] = _layer_norm(base_ref[...] + ffn, g2_ref[...], b2_ref[...])


def _combine_call(code_flat, pad_start, base, route, g2, b2, ys):
    n, d = base.shape
    tt = min(MOVE_TILE, n)
    steps = n // tt
    return pl.pallas_call(
        _combine_kernel,
        grid=(steps,),
        in_specs=[
            pl.BlockSpec((TOP_K * tt,), lambda i: (i,), memory_space=pltpu.SMEM),
            pl.BlockSpec((TOP_K * tt,), lambda i: (jnp.minimum(i + 1, steps - 1),),
                         memory_space=pltpu.SMEM),
            pl.BlockSpec(memory_space=pltpu.SMEM),
            pl.BlockSpec((tt, d), lambda i: (i, 0)),
            pl.BlockSpec((tt, LANES), lambda i: (i, 0)),
            pl.BlockSpec((1, d), lambda i: (0, 0)),
            pl.BlockSpec((1, d), lambda i: (0, 0)),
            pl.BlockSpec(memory_space=pl.ANY),
        ],
        out_specs=pl.BlockSpec((tt, d), lambda i: (i, 0)),
        out_shape=jax.ShapeDtypeStruct((n, d), F32),
        scratch_shapes=[pltpu.VMEM((2, TOP_K * tt, d), F32), pltpu.SemaphoreType.DMA((2,))],
        compiler_params=_cparams("arbitrary"),
        name="moe_combine",
    )(code_flat, code_flat, pad_start, base, route, g2, b2, ys)


def _layer(h, p_i, w_in, b_forget, w_out, g1, b1, w_group, b_group, w_router, b_router,
           w_gate, w_up, w_down, w_ple, w_ple_gate, g2, b2, alpha):
    bsz, seq, d = h.shape
    n = bsz * seq
    x2 = h.reshape(n, d)

    wr = jnp.pad(jnp.concatenate([w_group, w_router], axis=1).astype(F32),
                 ((0, 0), (0, LANES - N_GROUPS - N_EXPERTS)))
    wr_hi = wr.astype(BF16)
    wr = jnp.concatenate([wr_hi, (wr - wr_hi.astype(F32)).astype(BF16)], axis=1)
    br = jnp.pad(jnp.concatenate([b_group, b_router]).astype(F32),
                 (0, LANES - N_GROUPS - N_EXPERTS)).reshape(1, LANES)

    qsb, qfx, kaug, vt, cfmin = _proj_call(x2, w_in, b_forget, seq)
    per_tile = min(ROW_TILE, seq) // ATT_BLOCK
    cfmin = cfmin[:, :per_tile, :FOX_HEADS].reshape(bsz, seq // ATT_BLOCK, FOX_HEADS)
    cfmin = cfmin.transpose(0, 2, 1).reshape(-1)

    sb = _sb_call(qsb.reshape(bsz, seq, SB_COLS)).reshape(n, SB_WIDTH)
    fx = _fox_call(qfx.reshape(bsz, seq, FOX_WIDTH), kaug.reshape(bsz, seq, FOX_HEADS * LANES),
                   vt, cfmin).reshape(n, FOX_WIDTH)

    base, h1, route, counts = _post_call(
        sb, fx, x2, p_i.reshape(n, -1), w_out.astype(BF16), g1.reshape(1, d), b1.reshape(1, d),
        wr, br, w_ple_gate.astype(BF16), w_ple.astype(BF16), alpha)

    counts = counts[0, ROUTE_OFF:ROUTE_OFF + N_EXPERTS].astype(jnp.int32)
    padded = ((counts + MOE_BLOCK - 1) // MOE_BLOCK) * MOE_BLOCK
    pad_end = jnp.cumsum(padded)
    pad_start = pad_end - padded
    eid = route[:, 0:TOP_K].astype(jnp.int32)
    rank = route[:, 4:4 + TOP_K].astype(jnp.int32)
    code = (eid * (1 << RANK_BITS) + rank).reshape(n * TOP_K)
    cap = n * TOP_K + N_EXPERTS * MOE_BLOCK
    nblk = cap // MOE_BLOCK
    blk_first = jnp.arange(nblk, dtype=jnp.int32)[:, None] * MOE_BLOCK
    blk_expert = jnp.minimum(jnp.sum((pad_end[None, :] <= blk_first).astype(jnp.int32), axis=1),
                             N_EXPERTS - 1)
    n_used = (pad_end[-1:] // MOE_BLOCK).astype(jnp.int32)

    xs = _dispatch_call(h1, code, pad_start, jnp.zeros((cap, d), F32))
    ys = _expert_call(blk_expert, n_used, xs, w_gate, w_up, w_down)
    out = _combine_call(code, pad_start, base, route, g2.reshape(1, d), b2.reshape(1, d), ys)
    return out.reshape(bsz, seq, d)


def kernel(x, p, w_in, b_forget, w_out, ln_mix_g, ln_mix_b, w_group, b_group, w_router, b_router,
           w_gate, w_up, w_down, w_ple, w_ple_gate, ln_ffn_g, ln_ffn_b):
    depth = w_in.shape[0]
    alpha = (2 * depth) ** 0.25
    h = x
    for i in range(depth):
        h = _layer(h, p[i], w_in[i], b_forget[i], w_out[i], ln_mix_g[i], ln_mix_b[i], w_group[i],
                   b_group[i], w_router[i], b_router[i], w_gate[i], w_up[i], w_down[i], w_ple[i],
                   w_ple_gate[i], ln_ffn_g[i], ln_ffn_b[i], alpha)
    return h
```

```python
import functools

import jax
import jax.numpy as jnp
from jax import lax
from jax.experimental import pallas as pl
from jax.experimental.pallas import tpu as pltpu

F32 = jnp.float32
BF16 = jnp.bfloat16

HEAD_DIM = 64
SB_HEADS = 8
FOX_HEADS = 8
SB_WIDTH = SB_HEADS * HEAD_DIM
FOX_WIDTH = FOX_HEADS * HEAD_DIM
N_GROUPS = 4
EXPERTS_PER_GROUP = 8
N_EXPERTS = N_GROUPS * EXPERTS_PER_GROUP
TOP_K = 2
LN_EPS = 1e-5
SCALE = HEAD_DIM ** -0.5

LANES = 128
ATT_BLOCK = 256
ROW_TILE = 512
POST_TILE = 512
POST_SUB = 256
MOE_BLOCK = 256
MOVE_TILE = 256
DISPATCH_TILE = 512
ROUTE_OFF = N_GROUPS
VMEM_LIMIT = 56 * 1024 * 1024


def _cparams(*sem):
    return pltpu.CompilerParams(dimension_semantics=sem, vmem_limit_bytes=VMEM_LIMIT)


def _dot(a, b):
    return jnp.dot(a, b, preferred_element_type=F32)


def _dot_nt(a, b):
    return lax.dot_general(a, b, (((1,), (1,)), ((), ())), preferred_element_type=F32)


def _split2(x):
    hi = x.astype(BF16)
    lo = (x - hi.astype(F32)).astype(BF16)
    return hi, lo


def _split3(x):
    hi = x.astype(BF16)
    r = x - hi.astype(F32)
    mid = r.astype(BF16)
    lo = (r - mid.astype(F32)).astype(BF16)
    return hi, mid, lo


def _layer_norm(u, g, b):
    mu = jnp.mean(u, axis=-1, keepdims=True)
    d = u - mu
    var = jnp.mean(d * d, axis=-1, keepdims=True)
    return d * lax.rsqrt(var + LN_EPS) * g + b


SB_COLS = 3 * SB_WIDTH
FXK_OFF = SB_COLS + FOX_WIDTH
FGATE_OFF = FXK_OFF + FOX_HEADS * LANES
PROJ_COLS = FGATE_OFF + LANES
AUG_LANE = HEAD_DIM
AUG_TERMS = 3


def _proj_kernel(x_ref, w_ref, wvt_ref, bf_ref, tri_ref, sel_ref, qsb_ref, qfx_ref, kaug_ref,
                 vt_ref, cfmin_ref, carry_ref, *, tiles_per_seq):
    i = pl.program_id(0)
    t = ATT_BLOCK

    @pl.when(i % tiles_per_seq == 0)
    def _():
        carry_ref[...] = jnp.zeros_like(carry_ref)

    xb = x_ref[...].astype(BF16)
    tm = xb.shape[0]
    chunk = 512
    for c in range(SB_COLS // chunk):
        sl = slice(c * chunk, (c + 1) * chunk)
        res = _dot(xb, w_ref[:, sl])
        if SB_WIDTH <= c * chunk < 2 * SB_WIDTH:
            res = res * LOG2E
        qsb_ref[:, sl] = res.astype(BF16)
    qfx_ref[...] = _dot(xb, w_ref[:, SB_COLS:FXK_OFF]).astype(BF16)

    fl = _dot(xb, w_ref[:, FGATE_OFF:]) + bf_ref[...]
    lf = jnp.minimum(fl, 0.0) - jnp.log(1.0 + jnp.exp(-jnp.abs(fl)))
    tri = tri_ref[...]
    hi, mid, lo = _split3(lf)
    cum = _dot(tri, hi) + _dot(tri, mid) + _dot(tri, lo) + carry_ref[...]
    carry_ref[...] = cum[-1:, :]

    cum2 = cum * LOG2E
    nhi, nmid, nlo = _split3(-cum2)
    lane = lax.broadcasted_iota(jnp.int32, (tm, LANES), 1)
    terms = jnp.where(lane < FOX_HEADS, nhi, jnp.where(lane < 2 * FOX_HEADS, nmid, nlo))
    aug = _dot(terms, sel_ref[...])
    for c in range((FGATE_OFF - FXK_OFF) // chunk):
        sl = slice(c * chunk, (c + 1) * chunk)
        wsl = slice(FXK_OFF + c * chunk, FXK_OFF + (c + 1) * chunk)
        kaug_ref[:, sl] = (_dot(xb, w_ref[:, wsl]) * LOG2E + aug[:, sl]).astype(BF16)

    mins = []
    for c in range(tm // t):
        rows = slice(c * t, (c + 1) * t)
        vt_ref[c] = _dot_nt(wvt_ref[...], xb[rows, :]).astype(BF16)
        mins.append(jnp.min(cum2[rows, :], axis=0, keepdims=True))
    mins.append(jnp.zeros((8 - tm // t, LANES), F32))
    cfmin_ref[0] = jnp.concatenate(mins, axis=0)


def _proj_weights(w_in, b_forget):
    d = w_in.shape[0]
    off_q_fx = SB_COLS
    off_k_fx = off_q_fx + FOX_WIDTH
    off_v_fx = off_k_fx + FOX_WIDTH
    off_f = off_v_fx + FOX_WIDTH
    wk = w_in[:, off_k_fx:off_v_fx].reshape(d, FOX_HEADS, HEAD_DIM)
    wk = jnp.pad(wk, ((0, 0), (0, 0), (0, LANES - HEAD_DIM))).reshape(d, FOX_HEADS * LANES)
    wf = jnp.tile(w_in[:, off_f:off_f + FOX_HEADS], (1, AUG_TERMS))
    wf = jnp.pad(wf, ((0, 0), (0, LANES - AUG_TERMS * FOX_HEADS)))
    w_pad = jnp.concatenate([w_in[:, :off_k_fx], wk, wf], axis=1).astype(BF16)
    wvt = w_in[:, off_v_fx:off_f].T.astype(BF16)
    bf = jnp.pad(jnp.tile(b_forget.astype(F32), AUG_TERMS), (0, LANES - AUG_TERMS * FOX_HEADS))
    src = lax.broadcasted_iota(jnp.int32, (LANES, FOX_HEADS * LANES), 0)
    dst = lax.broadcasted_iota(jnp.int32, (LANES, FOX_HEADS * LANES), 1)
    head, term = src % FOX_HEADS, src // FOX_HEADS
    sel = ((term < AUG_TERMS) & (dst == head * LANES + AUG_LANE + term)).astype(BF16)
    return w_pad, wvt, bf.reshape(1, LANES), sel


def _proj_call(x2, w_in, b_forget, seq):
    n, d = x2.shape
    tm = min(ROW_TILE, seq)
    t = ATT_BLOCK
    w_pad, wvt, bf_pad, sel = _proj_weights(w_in, b_forget)
    row = lax.broadcasted_iota(jnp.int32, (tm, tm), 0)
    col = lax.broadcasted_iota(jnp.int32, (tm, tm), 1)
    tri = (col <= row).astype(BF16)
    const = lambda i: (0, 0)
    tile = lambda i: (i, 0)
    return pl.pallas_call(
        functools.partial(_proj_kernel, tiles_per_seq=seq // tm),
        grid=(n // tm,),
        in_specs=[
            pl.BlockSpec((tm, d), tile),
            pl.BlockSpec((d, PROJ_COLS), const),
            pl.BlockSpec((FOX_WIDTH, d), const),
            pl.BlockSpec((1, LANES), const),
            pl.BlockSpec((tm, tm), const),
            pl.BlockSpec(sel.shape, const),
        ],
        out_specs=[
            pl.BlockSpec((tm, SB_COLS), tile),
            pl.BlockSpec((tm, FOX_WIDTH), tile),
            pl.BlockSpec((tm, FOX_HEADS * LANES), tile),
            pl.BlockSpec((tm // t, FOX_WIDTH, t), lambda i: (i, 0, 0)),
            pl.BlockSpec((1, 8, LANES), lambda i: (i, 0, 0)),
        ],
        out_shape=[
            jax.ShapeDtypeStruct((n, SB_COLS), BF16),
            jax.ShapeDtypeStruct((n, FOX_WIDTH), BF16),
            jax.ShapeDtypeStruct((n, FOX_HEADS * LANES), BF16),
            jax.ShapeDtypeStruct((n // t, FOX_WIDTH, t), BF16),
            jax.ShapeDtypeStruct((n // tm, 8, LANES), F32),
        ],
        scratch_shapes=[pltpu.VMEM((1, LANES), F32)],
        compiler_params=_cparams("arbitrary"),
        name="proj",
    )(x2, w_pad, wvt, bf_pad, tri, sel)


ZERO_EXP2 = 160.0
LOG2E = 1.4426950408889634


def _softplus2(z2):
    return jnp.maximum(z2, 0.0) + jnp.log2(1.0 + jnp.exp2(-jnp.abs(z2)))


def _key_absmax(k_ref, kam_ref):
    seq, width = k_ref.shape[1], k_ref.shape[2]
    rows = min(512, seq)

    def body(i, m):
        blk = k_ref[0, pl.ds(pl.multiple_of(i * rows, rows), rows), :]
        return jnp.maximum(m, jnp.max(jnp.abs(blk.astype(F32)), axis=0, keepdims=True))

    kam = lax.fori_loop(0, seq // rows, body, jnp.zeros((1, width), F32))
    kam_ref[...] = jnp.broadcast_to(kam, kam_ref.shape)


def _score_bound(qh, kam):
    zmax = jnp.sum(jnp.abs(qh.astype(F32)) * kam, axis=-1, keepdims=True)
    return zmax * (1.0 + 2.0 ** -10) + 2.0 ** -10


SB_QBLOCKS = 2


def _sb_kernel(q_ref, k_ref, v_ref, tri_ref, o_ref, kam_ref):
    t = ATT_BLOCK
    step = pl.program_id(2)

    @pl.when(step == 0)
    def _():
        _key_absmax(k_ref, kam_ref)

    lane = lax.broadcasted_iota(jnp.int32, (t, LANES), 1)
    row = lax.broadcasted_iota(jnp.int32, (t, t), 0)
    col = lax.broadcasted_iota(jnp.int32, (t, t), 1)
    earlier = col < row
    tri = tri_ref[...]
    kam = kam_ref[0:1, :]

    def block(qs, kb, state, diag, valid=None):
        start = pl.multiple_of(kb * t, t)
        kblk = k_ref[0, pl.ds(start, t), :]
        vblk = v_ref[0, pl.ds(start, t), :]
        new = []
        for h in range(2):
            carry, acc = state[2 * h], state[2 * h + 1]
            z = _dot_nt(qs[h], kblk)
            sp = _softplus2(z)
            if diag:
                sp = jnp.where(earlier, sp, 0.0)
            if valid is not None:
                sp = jnp.where(valid, sp, 0.0)
            hi, lo = _split2(sp)
            cum = _dot(hi, tri) + _dot(lo, tri)
            w = jnp.exp2(z - cum - carry)
            if diag:
                w = jnp.where(earlier, w, 0.0)
            if valid is not None:
                w = jnp.where(valid, w, 0.0)
            acc = acc + _dot(w.astype(BF16), vblk)
            carry = carry + cum[:, 0:1]
            new += [carry, acc]
        return tuple(new)

    init = (jnp.zeros((t, 1), F32), jnp.zeros((t, LANES), F32)) * 2
    streams = []
    for x in range(SB_QBLOCKS):
        qi = step * SB_QBLOCKS + x
        q = q_ref[0, x * t:(x + 1) * t, :] * jnp.asarray(SCALE, BF16)
        zero_q = jnp.zeros_like(q)
        qs = (jnp.where(lane < HEAD_DIM, q, zero_q), jnp.where(lane >= HEAD_DIM, q, zero_q))
        zmax = tuple(_score_bound(qh, kam) for qh in qs)
        state = block(qs, qi, init, True)
        state = block(qs, jnp.maximum(qi - 1, 0), state, False, valid=(qi >= 1) if x == 0 else None)
        streams.append((qi, qs, zmax, state))

    for x, (qi, qs, zmax, state) in enumerate(streams):
        def live(state, zmax=zmax):
            worst = jnp.maximum(jnp.max(zmax[0] - state[0]), jnp.max(zmax[1] - state[2]))
            return worst > -ZERO_EXP2

        def cond(c):
            return (c[0] >= 0) & c[1]

        def body(c, qs=qs, live=live):
            state = block(qs, c[0], c[2:], False)
            return (c[0] - 1, live(state)) + state

        c = lax.while_loop(cond, body, (qi - 2, live(state)) + state)
        o_ref[0, x * t:(x + 1) * t, :] = jnp.where(lane < HEAD_DIM, c[3], c[5]).astype(o_ref.dtype)


def _att_tri():
    t = ATT_BLOCK
    s = lax.broadcasted_iota(jnp.int32, (t, t), 0)
    j = lax.broadcasted_iota(jnp.int32, (t, t), 1)
    return (s >= j).astype(BF16)


def _sb_call(qkv):
    b, seq, _ = qkv.shape
    t = ATT_BLOCK
    pairs = SB_WIDTH // LANES
    q_off, k_off, v_off = 0, SB_WIDTH // LANES, 2 * SB_WIDTH // LANES
    rows = SB_QBLOCKS * t
    return pl.pallas_call(
        _sb_kernel,
        grid=(b, pairs, seq // rows),
        in_specs=[
            pl.BlockSpec((1, rows, LANES), lambda bi, hp, qi: (bi, qi, q_off + hp)),
            pl.BlockSpec((1, seq, LANES), lambda bi, hp, qi: (bi, 0, k_off + hp)),
            pl.BlockSpec((1, seq, LANES), lambda bi, hp, qi: (bi, 0, v_off + hp)),
            pl.BlockSpec((t, t), lambda bi, hp, qi: (0, 0)),
        ],
        out_specs=pl.BlockSpec((1, rows, LANES), lambda bi, hp, qi: (bi, qi, hp)),
        out_shape=jax.ShapeDtypeStruct((b, seq, SB_WIDTH), BF16),
        scratch_shapes=[pltpu.VMEM((8, LANES), F32)],
        compiler_params=_cparams("arbitrary", "arbitrary", "arbitrary"),
        name="sb_attn",
    )(qkv, qkv, qkv, _att_tri())


FOX_QBLOCKS = 2


def _fox_kernel(cfmin_ref, q_ref, k_ref, vt_ref, o_ref, s_a, s_b, p_a, p_b, kam_ref, *, nblk):
    t = ATT_BLOCK
    bi, hp, step = pl.program_id(0), pl.program_id(1), pl.program_id(2)

    @pl.when(step == 0)
    def _():
        _key_absmax(k_ref, kam_ref)

    lane = lax.broadcasted_iota(jnp.int32, (t, LANES), 1)
    feat = lax.broadcasted_iota(jnp.int32, (LANES, t), 0)
    key = lax.broadcasted_iota(jnp.int32, (t, t), 0)
    qry = lax.broadcasted_iota(jnp.int32, (t, t), 1)
    visible = key <= qry
    ones_aug = jnp.where(lane < AUG_LANE + AUG_TERMS, 1.0, 0.0)
    own_rows = (feat < HEAD_DIM, feat >= HEAD_DIM)
    neg_inf = jnp.full((t, t), -jnp.inf, F32)

    chains = [(x, h) for x in range(FOX_QBLOCKS) for h in range(2)]
    qis = [step * FOX_QBLOCKS + x for x in range(FOX_QBLOCKS)]
    qt, zmax = [], []
    for x, h in chains:
        qf = q_ref[0, x * t:(x + 1) * t, :].astype(F32) * SCALE
        qh = qf if h == 0 else pltpu.roll(qf, HEAD_DIM, 1)
        qht = jnp.where(lane < HEAD_DIM, qh, ones_aug).T
        qt.append(qht.astype(BF16))
        kam = kam_ref[0:1, h * LANES:(h + 1) * LANES] * (1.0 + 2.0 ** -7)
        kam16 = jnp.broadcast_to(jnp.where(lane[0:1, :] < HEAD_DIM, kam, 0.0), (16, LANES)).astype(BF16)
        aq = jnp.where(feat < HEAD_DIM, jnp.abs(qht), 0.0).astype(BF16)
        zmax.append(_dot(kam16, aq)[0:1, :] * (1.0 + 2.0 ** -10) + 2.0 ** -10)

    def qk(c, off):
        h = chains[c][1]
        kb = jnp.maximum(qis[chains[c][0]] - off, 0)
        start = pl.multiple_of(kb * t, t)
        return _dot(k_ref[0, pl.ds(start, t), h * LANES:(h + 1) * LANES], qt[c])

    def softmax_part(s, m):
        m_new = jnp.maximum(m, jnp.max(s, axis=0, keepdims=True))
        return m_new, jnp.exp2(s - m_new).astype(BF16), jnp.exp2(m - m_new)

    def pv(c, off, alpha, acc, p_buf):
        vt = vt_ref[jnp.maximum(qis[chains[c][0]] - off, 0)]
        vmod = jnp.where(own_rows[chains[c][1]], vt, jnp.ones_like(vt))
        return alpha * acc + _dot(vmod, p_buf[c])

    def live(off, ms):
        alive = None
        for c, (x, h) in enumerate(chains):
            kb = qis[x] - off
            cf_min = cfmin_ref[(bi * FOX_HEADS + 2 * hp + h) * nblk + jnp.maximum(kb, 0)]
            a_c = (kb >= 0) & (jnp.max(zmax[c] - ms[c]) - cf_min > -ZERO_EXP2)
            alive = a_c if alive is None else alive | a_c
        return alive

    def scores(c, off, raw):
        return jnp.where(qis[chains[c][0]] - off >= 0, raw, neg_inf)

    n = len(chains)
    s_bufs, p_bufs = (s_a, s_b), (p_a, p_b)
    ms, alphas = [], []
    for c in range(n):
        m, p, alpha = softmax_part(jnp.where(visible, qk(c, 0), neg_inf), jnp.full((1, t), -jnp.inf, F32))
        p_a[c] = p
        s_b[c] = qk(c, 1)
        ms.append(m)
        alphas.append(alpha)
    accs = [jnp.zeros((LANES, t), F32)] * n

    def cond(st):
        return (st[0] <= qis[-1]) & st[1]

    def body(st):
        j = st[0]
        ms, alphas, accs = list(st[2:2 + n]), list(st[2 + n:2 + 2 * n]), list(st[2 + 2 * n:])
        for half in range(2):
            off = j + half
            p_old, p_new = p_bufs[half], p_bufs[1 - half]
            s_cur, s_nxt = s_bufs[1 - half], s_bufs[half]
            for c in range(n):
                accs[c] = pv(c, off - 1, alphas[c], accs[c], p_old)
            for c in range(n):
                ms[c], p, alphas[c] = softmax_part(scores(c, off, s_cur[c]), ms[c])
                p_new[c] = p
            for c in range(n):
                s_nxt[c] = qk(c, off + 1)
        return (j + 2, live(j + 2, ms)) + tuple(ms) + tuple(alphas) + tuple(accs)

    st = lax.while_loop(cond, body, (jnp.int32(1), live(1, ms)) + tuple(ms) + tuple(alphas) + tuple(accs))
    j = st[0]
    accs = [pv(c, j - 1, st[2 + n + c], st[2 + 2 * n + c], p_a) for c in range(n)]
    for x in range(FOX_QBLOCKS):
        acc0, acc1 = accs[2 * x], accs[2 * x + 1]
        out_t = jnp.concatenate([acc0[:HEAD_DIM] / acc0[HEAD_DIM:HEAD_DIM + 1],
                                 acc1[HEAD_DIM:] / acc1[0:1]], axis=0)
        o_ref[0, x * t:(x + 1) * t, :] = out_t.T.astype(o_ref.dtype)


def _fox_call(qfx, kaug, vt, cfmin):
    b, seq, _ = qfx.shape
    t = ATT_BLOCK
    nblk = seq // t
    rows = FOX_QBLOCKS * t
    chains = 2 * FOX_QBLOCKS
    pairs = FOX_WIDTH // LANES
    return pl.pallas_call(
        functools.partial(_fox_kernel, nblk=nblk),
        grid=(b, pairs, seq // rows),
        in_specs=[
            pl.BlockSpec(memory_space=pltpu.SMEM),
            pl.BlockSpec((1, rows, LANES), lambda bi, hp, qi: (bi, qi, hp)),
            pl.BlockSpec((1, seq, 2 * LANES), lambda bi, hp, qi: (bi, 0, hp)),
            pl.BlockSpec((nblk, LANES, t), lambda bi, hp, qi: (bi, hp, 0)),
        ],
        out_specs=pl.BlockSpec((1, rows, LANES), lambda bi, hp, qi: (bi, qi, hp)),
        out_shape=jax.ShapeDtypeStruct((b, seq, FOX_WIDTH), BF16),
        scratch_shapes=[pltpu.VMEM((chains, t, t), F32), pltpu.VMEM((chains, t, t), F32),
                        pltpu.VMEM((chains, t, t), BF16), pltpu.VMEM((chains, t, t), BF16),
                        pltpu.VMEM((8, 2 * LANES), F32)],
        compiler_params=_cparams("arbitrary", "arbitrary", "arbitrary"),
        name="fox_attn",
    )(cfmin, qfx, kaug, vt)


def _post_kernel(sb_ref, fx_ref, x_ref, p_ref, wo_ref, g1_ref, b1_ref, wr_ref, br_ref, wpg_ref,
                 wp_ref, tri_ref, base_ref, h_ref, route_ref, cnt_ref, carry_ref, *, alpha):
    i = pl.program_id(0)

    @pl.when(i == 0)
    def _():
        carry_ref[...] = jnp.zeros_like(carry_ref)

    tm = POST_SUB
    for half in range(x_ref.shape[0] // tm):
        rows = slice(half * tm, (half + 1) * tm)
        _post_rows(sb_ref[rows, :], fx_ref[rows, :], x_ref[rows, :], p_ref[rows, :], wo_ref, g1_ref,
                   b1_ref, wr_ref, br_ref, wpg_ref, wp_ref, tri_ref, base_ref.at[rows, :],
                   h_ref.at[rows, :], route_ref.at[rows, :], cnt_ref, carry_ref, alpha)


def _post_rows(sb, fx, x, p, wo_ref, g1_ref, b1_ref, wr_ref, br_ref, wpg_ref, wp_ref, tri_ref,
               base_ref, h_ref, route_ref, cnt_ref, carry_ref, alpha):
    mix = _dot(sb, wo_ref[:SB_WIDTH, :]) + _dot(fx, wo_ref[SB_WIDTH:, :])
    h = _layer_norm(alpha * x + mix, g1_ref[...], b1_ref[...])
    h_ref[...] = h

    tm = h.shape[0]
    lane = lax.broadcasted_iota(jnp.int32, (tm, LANES), 1)
    big = jnp.int32(4 * LANES)
    h_hi, h_lo = _split2(h)
    hw = _dot(h_hi, wr_ref[...])
    logits = hw[:, :LANES] + hw[:, LANES:] + _dot(h_lo, wr_ref[:, :LANES]) + br_ref[...]

    def first_argmax(vals):
        vmax = jnp.max(vals, axis=-1, keepdims=True)
        idx = jnp.min(jnp.where(vals == vmax, lane, big), axis=-1, keepdims=True)
        return vmax, idx

    neg = -jnp.inf
    gl = jnp.where(lane < N_GROUPS, logits, neg)
    gmax, gsel = first_argmax(gl)
    g_prob = 1.0 / jnp.sum(jnp.exp(gl - gmax), axis=-1, keepdims=True)
    lo = ROUTE_OFF + gsel * EXPERTS_PER_GROUP
    el = jnp.where((lane >= lo) & (lane < lo + EXPERTS_PER_GROUP), logits, neg)
    v1, i1 = first_argmax(el)
    v2, i2 = first_argmax(jnp.where(lane == i1, neg, el))
    e2 = jnp.exp(v2 - v1)
    gate1 = g_prob / (1.0 + e2)
    gate2 = g_prob * e2 / (1.0 + e2)

    hit1 = lane == i1
    hit2 = lane == i2
    onehot = jnp.where(hit1 | hit2, 1.0, 0.0)
    before = _dot(tri_ref[...], onehot.astype(BF16)) + carry_ref[...]
    rank1 = jnp.sum(jnp.where(hit1, before, 0.0), axis=-1, keepdims=True)
    rank2 = jnp.sum(jnp.where(hit2, before, 0.0), axis=-1, keepdims=True)
    total = before[-1:, :] + onehot[-1:, :]
    carry_ref[...] = total
    cnt_ref[...] = jnp.broadcast_to(total, cnt_ref.shape)

    route = jnp.zeros((tm, LANES), F32)
    fields = ((i1 - ROUTE_OFF).astype(F32), (i2 - ROUTE_OFF).astype(F32), gate1, gate2, rank1, rank2)
    for k, val in enumerate(fields):
        route = jnp.where(lane == k, val, route)
    route_ref[...] = route

    ple = _dot(p.astype(BF16), wp_ref[...])
    gate = jax.nn.sigmoid(_dot(h_hi, wpg_ref[...]))
    base_ref[...] = alpha * h + ple * gate


def _post_call(sb, fx, x2, p2, wo, g1, b1, wr, br, wpg, wp, alpha):
    n, d = x2.shape
    tm = min(POST_TILE, n)
    half = POST_SUB
    row = lax.broadcasted_iota(jnp.int32, (half, half), 0)
    col = lax.broadcasted_iota(jnp.int32, (half, half), 1)
    tri = (col < row).astype(BF16)
    const = lambda i: (0, 0)
    tile = lambda i: (i, 0)
    return pl.pallas_call(
        functools.partial(_post_kernel, alpha=alpha),
        grid=(n // tm,),
        in_specs=[
            pl.BlockSpec((tm, SB_WIDTH), tile),
            pl.BlockSpec((tm, FOX_WIDTH), tile),
            pl.BlockSpec((tm, d), tile),
            pl.BlockSpec((tm, p2.shape[1]), tile),
            pl.BlockSpec(wo.shape, const),
            pl.BlockSpec((1, d), const),
            pl.BlockSpec((1, d), const),
            pl.BlockSpec(wr.shape, const),
            pl.BlockSpec((1, LANES), const),
            pl.BlockSpec(wpg.shape, const),
            pl.BlockSpec(wp.shape, const),
            pl.BlockSpec(tri.shape, const),
        ],
        out_specs=[
            pl.BlockSpec((tm, d), tile),
            pl.BlockSpec((tm, d), tile),
            pl.BlockSpec((tm, LANES), tile),
            pl.BlockSpec((8, LANES), const),
        ],
        out_shape=[
            jax.ShapeDtypeStruct((n, d), F32),
            jax.ShapeDtypeStruct((n, d), F32),
            jax.ShapeDtypeStruct((n, LANES), F32),
            jax.ShapeDtypeStruct((8, LANES), F32),
        ],
        scratch_shapes=[pltpu.VMEM((1, LANES), F32)],
        compiler_params=_cparams("arbitrary"),
        name="post_attn",
    )(sb, fx, x2, p2, wo, g1, b1, wr, br, wpg, wp, tri)


RANK_BITS = 20
ISSUE_UNROLL = 8


def _slot_dest(code_ref, start_ref, idx):
    code = code_ref[idx]
    return start_ref[lax.shift_right_logical(code, RANK_BITS)] + (code & ((1 << RANK_BITS) - 1))


def _dispatch_kernel(code_ref, start_ref, h_ref, xs_in_ref, xs_ref, sem, *, tt):
    del xs_in_ref

    def copy(g, u, k):
        dest = _slot_dest(code_ref, start_ref, TOP_K * (g * ISSUE_UNROLL + u) + k)
        return pltpu.make_async_copy(h_ref.at[g, pl.ds(u, 1), :], xs_ref.at[pl.ds(dest, 1), :], sem)

    def start(g, c):
        for u in range(ISSUE_UNROLL):
            for k in range(TOP_K):
                copy(g, u, k).start()
        return c

    def wait(g, c):
        for u in range(ISSUE_UNROLL):
            for k in range(TOP_K):
                copy(g, u, k).wait()
        return c

    lax.fori_loop(0, tt // ISSUE_UNROLL, start, 0)
    lax.fori_loop(0, tt // ISSUE_UNROLL, wait, 0)


def _dispatch_call(h, code_flat, pad_start, xs_zero):
    n, d = h.shape
    tt = min(DISPATCH_TILE, n)
    return pl.pallas_call(
        functools.partial(_dispatch_kernel, tt=tt),
        grid=(n // tt,),
        in_specs=[
            pl.BlockSpec((TOP_K * tt,), lambda i: (i,), memory_space=pltpu.SMEM),
            pl.BlockSpec(memory_space=pltpu.SMEM),
            pl.BlockSpec((tt // ISSUE_UNROLL, ISSUE_UNROLL, d), lambda i: (i, 0, 0)),
            pl.BlockSpec(memory_space=pl.ANY),
        ],
        out_specs=pl.BlockSpec(memory_space=pl.ANY),
        out_shape=jax.ShapeDtypeStruct(xs_zero.shape, xs_zero.dtype),
        scratch_shapes=[pltpu.SemaphoreType.DMA(())],
        input_output_aliases={3: 0},
        compiler_params=_cparams("arbitrary"),
        name="moe_dispatch",
    )(code_flat, pad_start, h.reshape(n // ISSUE_UNROLL, ISSUE_UNROLL, d), xs_zero)


def _expert_kernel(be_ref, nb_ref, xs_ref, wg_ref, wu_ref, wd_ref, ys_ref, wg_bf, wu_bf, wd_bf):
    i = pl.program_id(0)
    used = i < nb_ref[0]

    @pl.when(used & ((i == 0) | (be_ref[i] != be_ref[jnp.maximum(i - 1, 0)])))
    def _():
        wg_bf[...] = wg_ref[0].astype(BF16)
        wu_bf[...] = wu_ref[0].astype(BF16)
        wd_bf[...] = wd_ref[0].astype(BF16)

    @pl.when(used)
    def _():
        xb = xs_ref[...].astype(BF16)
        g = _dot(xb, wg_bf[...])
        u = _dot(xb, wu_bf[...])
        hdn = (g * jax.nn.sigmoid(g)) * u
        ys_ref[...] = _dot(hdn.astype(BF16), wd_bf[...])

    @pl.when(jnp.logical_not(used))
    def _():
        ys_ref[...] = jnp.zeros_like(ys_ref)


def _expert_call(blk_expert, n_used, xs, wg, wu, wd):
    cap, d = xs.shape
    f = wg.shape[2]
    nblk = cap // MOE_BLOCK
    return pl.pallas_call(
        _expert_kernel,
        grid_spec=pltpu.PrefetchScalarGridSpec(
            num_scalar_prefetch=2,
            grid=(nblk,),
            in_specs=[
                pl.BlockSpec((MOE_BLOCK, d), lambda i, be, nb: (i, 0)),
                pl.BlockSpec((1, d, f), lambda i, be, nb: (be[i], 0, 0)),
                pl.BlockSpec((1, d, f), lambda i, be, nb: (be[i], 0, 0)),
                pl.BlockSpec((1, f, d), lambda i, be, nb: (be[i], 0, 0)),
            ],
            out_specs=pl.BlockSpec((MOE_BLOCK, d), lambda i, be, nb: (i, 0)),
            scratch_shapes=[pltpu.VMEM((d, f), BF16), pltpu.VMEM((d, f), BF16),
                            pltpu.VMEM((f, d), BF16)],
        ),
        out_shape=jax.ShapeDtypeStruct((cap, d), F32),
        compiler_params=_cparams("arbitrary"),
        name="moe_experts",
    )(blk_expert, n_used, xs, wg, wu, wd)


def _combine_kernel(code_ref, next_code_ref, start_ref, base_ref, route_ref, g2_ref, b2_ref, ys_ref,
                    o_ref, buf_ref, sems):
    tt = base_ref.shape[0]
    groups = tt // ISSUE_UNROLL
    i = pl.program_id(0)
    last = pl.num_programs(0) - 1

    def copy(codes, buf, g, u, k):
        dest = _slot_dest(codes, start_ref, TOP_K * (g * ISSUE_UNROLL + u) + k)
        return pltpu.make_async_copy(ys_ref.at[pl.ds(dest, 1), :],
                                     buf_ref.at[buf, k * groups + g, pl.ds(u, 1), :], sems.at[buf])

    def for_all_rows(codes, buf, wait):
        def body(g, c):
            for u in range(ISSUE_UNROLL):
                for k in range(TOP_K):
                    cp = copy(codes, buf, g, u, k)
                    if wait:
                        cp.wait()
                    else:
                        cp.start(priority=k)
            return c
        lax.fori_loop(0, groups, body, 0)

    cur = i % 2

    @pl.when(i == 0)
    def _():
        for_all_rows(code_ref, 0, False)

    @pl.when(i < last)
    def _():
        for_all_rows(next_code_ref, 1 - cur, False)

    for_all_rows(code_ref, cur, True)

    route = route_ref[...]
    d = base_ref.shape[1]
    y0 = buf_ref[cur, 0:groups].reshape(tt, d)
    y1 = buf_ref[cur, groups:2 * groups].reshape(tt, d)
    ffn = route[:, 2:3] * y0 + route[:, 3:4] * y1
    o_ref[...] = _layer_norm(base_ref[...] + ffn, g2_ref[...], b2_ref[...])


def _combine_call(code_flat, pad_start, base, route, g2, b2, ys):
    n, d = base.shape
    tt = min(MOVE_TILE, n)
    steps = n // tt
    return pl.pallas_call(
        _combine_kernel,
        grid=(steps,),
        in_specs=[
            pl.BlockSpec((TOP_K * tt,), lambda i: (i,), memory_space=pltpu.SMEM),
            pl.BlockSpec((TOP_K * tt,), lambda i: (jnp.minimum(i + 1, steps - 1),),
                         memory_space=pltpu.SMEM),
            pl.BlockSpec(memory_space=pltpu.SMEM),
            pl.BlockSpec((tt, d), lambda i: (i, 0)),
            pl.BlockSpec((tt, LANES), lambda i: (i, 0)),
            pl.BlockSpec((1, d), lambda i: (0, 0)),
            pl.BlockSpec((1, d), lambda i: (0, 0)),
            pl.BlockSpec(memory_space=pl.ANY),
        ],
        out_specs=pl.BlockSpec((tt, d), lambda i: (i, 0)),
        out_shape=jax.ShapeDtypeStruct((n, d), F32),
        scratch_shapes=[pltpu.VMEM((2, TOP_K * tt // ISSUE_UNROLL, ISSUE_UNROLL, d), F32),
                        pltpu.SemaphoreType.DMA((2,))],
        compiler_params=_cparams("arbitrary"),
        name="moe_combine",
    )(code_flat, code_flat, pad_start, base, route, g2, b2, ys)


def _layer(h, p_i, w_in, b_forget, w_out, g1, b1, w_group, b_group, w_router, b_router,
           w_gate, w_up, w_down, w_ple, w_ple_gate, g2, b2, alpha):
    bsz, seq, d = h.shape
    n = bsz * seq
    x2 = h.reshape(n, d)

    wr = jnp.pad(jnp.concatenate([w_group, w_router], axis=1).astype(F32),
                 ((0, 0), (0, LANES - N_GROUPS - N_EXPERTS)))
    wr_hi = wr.astype(BF16)
    wr = jnp.concatenate([wr_hi, (wr - wr_hi.astype(F32)).astype(BF16)], axis=1)
    br = jnp.pad(jnp.concatenate([b_group, b_router]).astype(F32),
                 (0, LANES - N_GROUPS - N_EXPERTS)).reshape(1, LANES)

    qsb, qfx, kaug, vt, cfmin = _proj_call(x2, w_in, b_forget, seq)
    per_tile = min(ROW_TILE, seq) // ATT_BLOCK
    cfmin = cfmin[:, :per_tile, :FOX_HEADS].reshape(bsz, seq // ATT_BLOCK, FOX_HEADS)
    cfmin = cfmin.transpose(0, 2, 1).reshape(-1)

    sb = _sb_call(qsb.reshape(bsz, seq, SB_COLS)).reshape(n, SB_WIDTH)
    fx = _fox_call(qfx.reshape(bsz, seq, FOX_WIDTH), kaug.reshape(bsz, seq, FOX_HEADS * LANES),
                   vt, cfmin).reshape(n, FOX_WIDTH)

    base, h1, route, counts = _post_call(
        sb, fx, x2, p_i.reshape(n, -1), w_out.astype(BF16), g1.reshape(1, d), b1.reshape(1, d),
        wr, br, w_ple_gate.astype(BF16), w_ple.astype(BF16), alpha)

    counts = counts[0, ROUTE_OFF:ROUTE_OFF + N_EXPERTS].astype(jnp.int32)
    padded = ((counts + MOE_BLOCK - 1) // MOE_BLOCK) * MOE_BLOCK
    pad_end = jnp.cumsum(padded)
    pad_start = pad_end - padded
    eid = route[:, 0:TOP_K].astype(jnp.int32)
    rank = route[:, 4:4 + TOP_K].astype(jnp.int32)
    code = (eid * (1 << RANK_BITS) + rank).reshape(n * TOP_K)
    cap = n * TOP_K + N_EXPERTS * MOE_BLOCK
    nblk = cap // MOE_BLOCK
    blk_first = jnp.arange(nblk, dtype=jnp.int32)[:, None] * MOE_BLOCK
    blk_expert = jnp.minimum(jnp.sum((pad_end[None, :] <= blk_first).astype(jnp.int32), axis=1),
                             N_EXPERTS - 1)
    n_used = (pad_end[-1:] // MOE_BLOCK).astype(jnp.int32)

    xs = _dispatch_call(h1, code, pad_start, jnp.zeros((cap, d), F32))
    ys = _expert_call(blk_expert, n_used, xs, w_gate, w_up, w_down)
    out = _combine_call(code, pad_start, base, route, g2.reshape(1, d), b2.reshape(1, d), ys)
    return out.reshape(bsz, seq, d)


def kernel(x, p, w_in, b_forget, w_out, ln_mix_g, ln_mix_b, w_group, b_group, w_router, b_router,
           w_gate, w_up, w_down, w_ple, w_ple_gate, ln_ffn_g, ln_ffn_b):
    depth = w_in.shape[0]
    alpha = (2 * depth) ** 0.25
    h = x
    for i in range(depth):
        h = _layer(h, p[i], w_in[i], b_forget[i], w_out[i], ln_mix_g[i], ln_mix_b[i], w_group[i],
                   b_group[i], w_router[i], b_router[i], w_gate[i], w_up[i], w_down[i], w_ple[i],
                   w_ple_gate[i], ln_ffn_g[i], ln_ffn_b[i], alpha)
    return h
```

```python
import functools

import jax
import jax.numpy as jnp
from jax import lax
from jax.experimental import pallas as pl
from jax.experimental.pallas import tpu as pltpu

F32 = jnp.float32
BF16 = jnp.bfloat16

HEAD_DIM = 64
SB_HEADS = 8
FOX_HEADS = 8
SB_WIDTH = SB_HEADS * HEAD_DIM
FOX_WIDTH = FOX_HEADS * HEAD_DIM
N_GROUPS = 4
EXPERTS_PER_GROUP = 8
N_EXPERTS = N_GROUPS * EXPERTS_PER_GROUP
TOP_K = 2
LN_EPS = 1e-5
SCALE = HEAD_DIM ** -0.5

LANES = 128
ATT_BLOCK = 256
ROW_TILE = 512
POST_TILE = 512
POST_SUB = 256
MOE_BLOCK = 256
MOVE_TILE = 256
DISPATCH_TILE = 512
ROUTE_OFF = N_GROUPS
VMEM_LIMIT = 56 * 1024 * 1024


def _cparams(*sem):
    return pltpu.CompilerParams(dimension_semantics=sem, vmem_limit_bytes=VMEM_LIMIT)


def _dot(a, b):
    return jnp.dot(a, b, preferred_element_type=F32)


def _dot_nt(a, b):
    return lax.dot_general(a, b, (((1,), (1,)), ((), ())), preferred_element_type=F32)


def _split2(x):
    hi = x.astype(BF16)
    lo = (x - hi.astype(F32)).astype(BF16)
    return hi, lo


def _split3(x):
    hi = x.astype(BF16)
    r = x - hi.astype(F32)
    mid = r.astype(BF16)
    lo = (r - mid.astype(F32)).astype(BF16)
    return hi, mid, lo


def _layer_norm(u, g, b):
    mu = jnp.mean(u, axis=-1, keepdims=True)
    d = u - mu
    var = jnp.mean(d * d, axis=-1, keepdims=True)
    return d * lax.rsqrt(var + LN_EPS) * g + b


SB_COLS = 3 * SB_WIDTH
FXK_OFF = SB_COLS + FOX_WIDTH
FGATE_OFF = FXK_OFF + FOX_HEADS * LANES
PROJ_COLS = FGATE_OFF + LANES
AUG_LANE = HEAD_DIM
AUG_TERMS = 3


def _proj_kernel(x_ref, w_ref, wvt_ref, bf_ref, tri_ref, sel_ref, qsb_ref, qfx_ref, kaug_ref,
                 vt_ref, cfmin_ref, carry_ref, *, tiles_per_seq):
    i = pl.program_id(0)
    t = ATT_BLOCK

    @pl.when(i % tiles_per_seq == 0)
    def _():
        carry_ref[...] = jnp.zeros_like(carry_ref)

    xb = x_ref[...].astype(BF16)
    tm = xb.shape[0]
    chunk = 512
    for c in range(SB_COLS // chunk):
        sl = slice(c * chunk, (c + 1) * chunk)
        res = _dot(xb, w_ref[:, sl])
        if SB_WIDTH <= c * chunk < 2 * SB_WIDTH:
            res = res * LOG2E
        qsb_ref[:, sl] = res.astype(BF16)
    qfx_ref[...] = _dot(xb, w_ref[:, SB_COLS:FXK_OFF]).astype(BF16)

    fl = _dot(xb, w_ref[:, FGATE_OFF:]) + bf_ref[...]
    lf = jnp.minimum(fl, 0.0) - jnp.log(1.0 + jnp.exp(-jnp.abs(fl)))
    tri = tri_ref[...]
    hi, mid, lo = _split3(lf)
    cum = _dot(tri, hi) + _dot(tri, mid) + _dot(tri, lo) + carry_ref[...]
    carry_ref[...] = cum[-1:, :]

    cum2 = cum * LOG2E
    nhi, nmid, nlo = _split3(-cum2)
    lane = lax.broadcasted_iota(jnp.int32, (tm, LANES), 1)
    terms = jnp.where(lane < FOX_HEADS, nhi, jnp.where(lane < 2 * FOX_HEADS, nmid, nlo))
    aug = _dot(terms, sel_ref[...])
    for c in range((FGATE_OFF - FXK_OFF) // chunk):
        sl = slice(c * chunk, (c + 1) * chunk)
        wsl = slice(FXK_OFF + c * chunk, FXK_OFF + (c + 1) * chunk)
        kaug_ref[:, sl] = (_dot(xb, w_ref[:, wsl]) * LOG2E + aug[:, sl]).astype(BF16)

    mins = []
    for c in range(tm // t):
        rows = slice(c * t, (c + 1) * t)
        vt_ref[c] = _dot_nt(wvt_ref[...], xb[rows, :]).astype(BF16)
        mins.append(jnp.min(cum2[rows, :], axis=0, keepdims=True))
    mins.append(jnp.zeros((8 - tm // t, LANES), F32))
    cfmin_ref[0] = jnp.concatenate(mins, axis=0)


def _proj_weights(w_in, b_forget):
    d = w_in.shape[0]
    off_q_fx = SB_COLS
    off_k_fx = off_q_fx + FOX_WIDTH
    off_v_fx = off_k_fx + FOX_WIDTH
    off_f = off_v_fx + FOX_WIDTH
    wk = w_in[:, off_k_fx:off_v_fx].reshape(d, FOX_HEADS, HEAD_DIM)
    wk = jnp.pad(wk, ((0, 0), (0, 0), (0, LANES - HEAD_DIM))).reshape(d, FOX_HEADS * LANES)
    wf = jnp.tile(w_in[:, off_f:off_f + FOX_HEADS], (1, AUG_TERMS))
    wf = jnp.pad(wf, ((0, 0), (0, LANES - AUG_TERMS * FOX_HEADS)))
    w_pad = jnp.concatenate([w_in[:, :off_k_fx], wk, wf], axis=1).astype(BF16)
    wvt = w_in[:, off_v_fx:off_f].T.astype(BF16)
    bf = jnp.pad(jnp.tile(b_forget.astype(F32), AUG_TERMS), (0, LANES - AUG_TERMS * FOX_HEADS))
    src = lax.broadcasted_iota(jnp.int32, (LANES, FOX_HEADS * LANES), 0)
    dst = lax.broadcasted_iota(jnp.int32, (LANES, FOX_HEADS * LANES), 1)
    head, term = src % FOX_HEADS, src // FOX_HEADS
    sel = ((term < AUG_TERMS) & (dst == head * LANES + AUG_LANE + term)).astype(BF16)
    return w_pad, wvt, bf.reshape(1, LANES), sel


def _proj_call(x2, w_in, b_forget, seq):
    n, d = x2.shape
    tm = min(ROW_TILE, seq)
    t = ATT_BLOCK
    w_pad, wvt, bf_pad, sel = _proj_weights(w_in, b_forget)
    row = lax.broadcasted_iota(jnp.int32, (tm, tm), 0)
    col = lax.broadcasted_iota(jnp.int32, (tm, tm), 1)
    tri = (col <= row).astype(BF16)
    const = lambda i: (0, 0)
    tile = lambda i: (i, 0)
    return pl.pallas_call(
        functools.partial(_proj_kernel, tiles_per_seq=seq // tm),
        grid=(n // tm,),
        in_specs=[
            pl.BlockSpec((tm, d), tile),
            pl.BlockSpec((d, PROJ_COLS), const),
            pl.BlockSpec((FOX_WIDTH, d), const),
            pl.BlockSpec((1, LANES), const),
            pl.BlockSpec((tm, tm), const),
            pl.BlockSpec(sel.shape, const),
        ],
        out_specs=[
            pl.BlockSpec((tm, SB_COLS), tile),
            pl.BlockSpec((tm, FOX_WIDTH), tile),
            pl.BlockSpec((tm, FOX_HEADS * LANES), tile),
            pl.BlockSpec((tm // t, FOX_WIDTH, t), lambda i: (i, 0, 0)),
            pl.BlockSpec((1, 8, LANES), lambda i: (i, 0, 0)),
        ],
        out_shape=[
            jax.ShapeDtypeStruct((n, SB_COLS), BF16),
            jax.ShapeDtypeStruct((n, FOX_WIDTH), BF16),
            jax.ShapeDtypeStruct((n, FOX_HEADS * LANES), BF16),
            jax.ShapeDtypeStruct((n // t, FOX_WIDTH, t), BF16),
            jax.ShapeDtypeStruct((n // tm, 8, LANES), F32),
        ],
        scratch_shapes=[pltpu.VMEM((1, LANES), F32)],
        compiler_params=_cparams("arbitrary"),
        name="proj",
    )(x2, w_pad, wvt, bf_pad, tri, sel)


ZERO_EXP2 = 152.0
LOG2E = 1.4426950408889634


def _softplus2(z2):
    return jnp.maximum(z2, 0.0) + jnp.log2(1.0 + jnp.exp2(-jnp.abs(z2)))


def _key_absmax(k_ref, kam_ref):
    seq, width = k_ref.shape[1], k_ref.shape[2]
    rows = min(512, seq)

    def body(i, m):
        blk = k_ref[0, pl.ds(pl.multiple_of(i * rows, rows), rows), :]
        return jnp.maximum(m, jnp.max(jnp.abs(blk.astype(F32)), axis=0, keepdims=True))

    kam = lax.fori_loop(0, seq // rows, body, jnp.zeros((1, width), F32))
    kam_ref[...] = jnp.broadcast_to(kam, kam_ref.shape)


def _key_sqnorm_max(k_ref, out_ref):
    seq, width = k_ref.shape[1], k_ref.shape[2]
    rows = min(512, seq)
    groups = width // LANES
    lane = lax.broadcasted_iota(jnp.int32, (rows, LANES), 1)

    def body(i, m):
        new = []
        for g in range(groups):
            blk = k_ref[0, pl.ds(pl.multiple_of(i * rows, rows), rows), g * LANES:(g + 1) * LANES]
            sq = jnp.where(lane < HEAD_DIM, jnp.square(blk.astype(F32)), 0.0)
            norms = jnp.sum(sq, axis=-1, keepdims=True)
            new.append(jnp.maximum(m[g], jnp.max(norms, axis=0, keepdims=True)))
        return tuple(new)

    best = lax.fori_loop(0, seq // rows, body, (jnp.zeros((1, 1), F32),) * groups)
    for g in range(groups):
        out_ref[:, g * LANES:(g + 1) * LANES] = jnp.broadcast_to(best[g], (out_ref.shape[0], LANES))


def _score_bound(qh, kam):
    zmax = jnp.sum(jnp.abs(qh.astype(F32)) * kam, axis=-1, keepdims=True)
    return zmax * (1.0 + 2.0 ** -10) + 2.0 ** -10


SB_QBLOCKS = 2


def _sb_kernel(q_ref, k_ref, v_ref, tri_ref, o_ref, kam_ref):
    t = ATT_BLOCK
    step = pl.program_id(2)

    @pl.when(step == 0)
    def _():
        _key_absmax(k_ref, kam_ref)

    lane = lax.broadcasted_iota(jnp.int32, (t, LANES), 1)
    row = lax.broadcasted_iota(jnp.int32, (t, t), 0)
    col = lax.broadcasted_iota(jnp.int32, (t, t), 1)
    earlier = col < row
    tri = tri_ref[...]
    kam = kam_ref[0:1, :]

    def block(qs, kb, state, diag, valid=None):
        start = pl.multiple_of(kb * t, t)
        kblk = k_ref[0, pl.ds(start, t), :]
        vblk = v_ref[0, pl.ds(start, t), :]
        new = []
        for h in range(2):
            carry, acc = state[2 * h], state[2 * h + 1]
            z = _dot_nt(qs[h], kblk)
            sp = _softplus2(z)
            if diag:
                sp = jnp.where(earlier, sp, 0.0)
            if valid is not None:
                sp = jnp.where(valid, sp, 0.0)
            hi, lo = _split2(sp)
            cum = _dot(hi, tri) + _dot(lo, tri)
            w = jnp.exp2(z - cum - carry)
            if diag:
                w = jnp.where(earlier, w, 0.0)
            if valid is not None:
                w = jnp.where(valid, w, 0.0)
            acc = acc + _dot(w.astype(BF16), vblk)
            carry = carry + cum[:, 0:1]
            new += [carry, acc]
        return tuple(new)

    init = (jnp.zeros((t, 1), F32), jnp.zeros((t, LANES), F32)) * 2
    streams = []
    for x in range(SB_QBLOCKS):
        qi = step * SB_QBLOCKS + x
        q = q_ref[0, x * t:(x + 1) * t, :] * jnp.asarray(SCALE, BF16)
        zero_q = jnp.zeros_like(q)
        qs = (jnp.where(lane < HEAD_DIM, q, zero_q), jnp.where(lane >= HEAD_DIM, q, zero_q))
        zmax = tuple(_score_bound(qh, kam) for qh in qs)
        state = block(qs, qi, init, True)
        state = block(qs, jnp.maximum(qi - 1, 0), state, False, valid=(qi >= 1) if x == 0 else None)
        streams.append((qi, qs, zmax, state))

    for x, (qi, qs, zmax, state) in enumerate(streams):
        def live(state, zmax=zmax):
            worst = jnp.maximum(jnp.max(zmax[0] - state[0]), jnp.max(zmax[1] - state[2]))
            return worst > -ZERO_EXP2

        def cond(c):
            return (c[0] >= 0) & c[1]

        def body(c, qs=qs, live=live):
            state = block(qs, c[0], c[2:], False)
            return (c[0] - 1, live(state)) + state

        c = lax.while_loop(cond, body, (qi - 2, live(state)) + state)
        o_ref[0, x * t:(x + 1) * t, :] = jnp.where(lane < HEAD_DIM, c[3], c[5]).astype(o_ref.dtype)


def _att_tri():
    t = ATT_BLOCK
    s = lax.broadcasted_iota(jnp.int32, (t, t), 0)
    j = lax.broadcasted_iota(jnp.int32, (t, t), 1)
    return (s >= j).astype(BF16)


def _sb_call(qkv):
    b, seq, _ = qkv.shape
    t = ATT_BLOCK
    pairs = SB_WIDTH // LANES
    q_off, k_off, v_off = 0, SB_WIDTH // LANES, 2 * SB_WIDTH // LANES
    rows = SB_QBLOCKS * t
    return pl.pallas_call(
        _sb_kernel,
        grid=(b, pairs, seq // rows),
        in_specs=[
            pl.BlockSpec((1, rows, LANES), lambda bi, hp, qi: (bi, qi, q_off + hp)),
            pl.BlockSpec((1, seq, LANES), lambda bi, hp, qi: (bi, 0, k_off + hp)),
            pl.BlockSpec((1, seq, LANES), lambda bi, hp, qi: (bi, 0, v_off + hp)),
            pl.BlockSpec((t, t), lambda bi, hp, qi: (0, 0)),
        ],
        out_specs=pl.BlockSpec((1, rows, LANES), lambda bi, hp, qi: (bi, qi, hp)),
        out_shape=jax.ShapeDtypeStruct((b, seq, SB_WIDTH), BF16),
        scratch_shapes=[pltpu.VMEM((8, LANES), F32)],
        compiler_params=_cparams("arbitrary", "arbitrary", "arbitrary"),
        name="sb_attn",
    )(qkv, qkv, qkv, _att_tri())


FOX_QBLOCKS = 2


def _fox_kernel(cfmin_ref, q_ref, k_ref, vt_ref, o_ref, s_a, s_b, p_a, p_b, kam_ref, *, nblk):
    t = ATT_BLOCK
    bi, hp, step = pl.program_id(0), pl.program_id(1), pl.program_id(2)

    @pl.when(step == 0)
    def _():
        _key_sqnorm_max(k_ref, kam_ref)

    lane = lax.broadcasted_iota(jnp.int32, (t, LANES), 1)
    feat = lax.broadcasted_iota(jnp.int32, (LANES, t), 0)
    key = lax.broadcasted_iota(jnp.int32, (t, t), 0)
    qry = lax.broadcasted_iota(jnp.int32, (t, t), 1)
    visible = key <= qry
    ones_aug = jnp.where(lane < AUG_LANE + AUG_TERMS, 1.0, 0.0)
    own_rows = (feat < HEAD_DIM, feat >= HEAD_DIM)
    neg_inf = jnp.full((t, t), -jnp.inf, F32)

    chains = [(x, h) for x in range(FOX_QBLOCKS) for h in range(2)]
    qis = [step * FOX_QBLOCKS + x for x in range(FOX_QBLOCKS)]
    qt, zmax = [], []
    for x, h in chains:
        qf = q_ref[0, x * t:(x + 1) * t, :].astype(F32) * SCALE
        qh = qf if h == 0 else pltpu.roll(qf, HEAD_DIM, 1)
        qht = jnp.where(lane < HEAD_DIM, qh, ones_aug).T
        qt.append(qht.astype(BF16))
        qn2 = jnp.sum(jnp.where(feat < HEAD_DIM, qht * qht, 0.0), axis=0, keepdims=True)
        kn2 = kam_ref[0:1, h * LANES:h * LANES + 1]
        zmax.append(jnp.sqrt(qn2 * kn2) * (1.0 + 2.0 ** -7) + 2.0 ** -7)

    def qk(c, off):
        h = chains[c][1]
        kb = jnp.maximum(qis[chains[c][0]] - off, 0)
        start = pl.multiple_of(kb * t, t)
        return _dot(k_ref[0, pl.ds(start, t), h * LANES:(h + 1) * LANES], qt[c])

    def softmax_part(s, m):
        m_new = jnp.maximum(m, jnp.max(s, axis=0, keepdims=True))
        return m_new, jnp.exp2(s - m_new).astype(BF16), jnp.exp2(m - m_new)

    def pv(c, off, alpha, acc, p_buf):
        vt = vt_ref[jnp.maximum(qis[chains[c][0]] - off, 0)]
        vmod = jnp.where(own_rows[chains[c][1]], vt, jnp.ones_like(vt))
        return alpha * acc + _dot(vmod, p_buf[c])

    def live(off, ms):
        alive = None
        for c, (x, h) in enumerate(chains):
            kb = qis[x] - off
            cf_min = cfmin_ref[(bi * FOX_HEADS + 2 * hp + h) * nblk + jnp.maximum(kb, 0)]
            a_c = (kb >= 0) & (jnp.max(zmax[c] - ms[c]) - cf_min > -ZERO_EXP2)
            alive = a_c if alive is None else alive | a_c
        return alive

    def scores(c, off, raw):
        return jnp.where(qis[chains[c][0]] - off >= 0, raw, neg_inf)

    n = len(chains)
    s_bufs, p_bufs = (s_a, s_b), (p_a, p_b)
    ms, alphas = [], []
    for c in range(n):
        m, p, alpha = softmax_part(jnp.where(visible, qk(c, 0), neg_inf), jnp.full((1, t), -jnp.inf, F32))
        p_a[c] = p
        s_b[c] = qk(c, 1)
        ms.append(m)
        alphas.append(alpha)
    accs = [jnp.zeros((LANES, t), F32)] * n

    def cond(st):
        return (st[0] <= qis[-1]) & st[1]

    def body(st):
        j = st[0]
        ms, alphas, accs = list(st[2:2 + n]), list(st[2 + n:2 + 2 * n]), list(st[2 + 2 * n:])
        for half in range(2):
            off = j + half
            p_old, p_new = p_bufs[half], p_bufs[1 - half]
            s_cur, s_nxt = s_bufs[1 - half], s_bufs[half]
            for c in range(n):
                accs[c] = pv(c, off - 1, alphas[c], accs[c], p_old)
            for c in range(n):
                certain = half == 0 and chains[c][0] == FOX_QBLOCKS - 1
                s = s_cur[c] if certain else scores(c, off, s_cur[c])
                ms[c], p, alphas[c] = softmax_part(s, ms[c])
                p_new[c] = p
            for c in range(n):
                s_nxt[c] = qk(c, off + 1)
        return (j + 2, live(j + 2, ms)) + tuple(ms) + tuple(alphas) + tuple(accs)

    st = lax.while_loop(cond, body, (jnp.int32(1), live(1, ms)) + tuple(ms) + tuple(alphas) + tuple(accs))
    j = st[0]
    accs = [pv(c, j - 1, st[2 + n + c], st[2 + 2 * n + c], p_a) for c in range(n)]
    for x in range(FOX_QBLOCKS):
        acc0, acc1 = accs[2 * x], accs[2 * x + 1]
        out_t = jnp.concatenate([acc0[:HEAD_DIM] / acc0[HEAD_DIM:HEAD_DIM + 1],
                                 acc1[HEAD_DIM:] / acc1[0:1]], axis=0)
        o_ref[0, x * t:(x + 1) * t, :] = out_t.T.astype(o_ref.dtype)


def _fox_call(qfx, kaug, vt, cfmin):
    b, seq, _ = qfx.shape
    t = ATT_BLOCK
    nblk = seq // t
    rows = FOX_QBLOCKS * t
    chains = 2 * FOX_QBLOCKS
    pairs = FOX_WIDTH // LANES
    return pl.pallas_call(
        functools.partial(_fox_kernel, nblk=nblk),
        grid=(b, pairs, seq // rows),
        in_specs=[
            pl.BlockSpec(memory_space=pltpu.SMEM),
            pl.BlockSpec((1, rows, LANES), lambda bi, hp, qi: (bi, qi, hp)),
            pl.BlockSpec((1, seq, 2 * LANES), lambda bi, hp, qi: (bi, 0, hp)),
            pl.BlockSpec((nblk, LANES, t), lambda bi, hp, qi: (bi, hp, 0)),
        ],
        out_specs=pl.BlockSpec((1, rows, LANES), lambda bi, hp, qi: (bi, qi, hp)),
        out_shape=jax.ShapeDtypeStruct((b, seq, FOX_WIDTH), BF16),
        scratch_shapes=[pltpu.VMEM((chains, t, t), F32), pltpu.VMEM((chains, t, t), F32),
                        pltpu.VMEM((chains, t, t), BF16), pltpu.VMEM((chains, t, t), BF16),
                        pltpu.VMEM((8, 2 * LANES), F32)],
        compiler_params=_cparams("arbitrary", "arbitrary", "arbitrary"),
        name="fox_attn",
    )(cfmin, qfx, kaug, vt)


def _post_kernel(sb_ref, fx_ref, x_ref, p_ref, wo_ref, g1_ref, b1_ref, wr_ref, br_ref, wpg_ref,
                 wp_ref, tri_ref, base_ref, h_ref, route_ref, cnt_ref, carry_ref, *, alpha):
    i = pl.program_id(0)

    @pl.when(i == 0)
    def _():
        carry_ref[...] = jnp.zeros_like(carry_ref)

    tm = POST_SUB
    for half in range(x_ref.shape[0] // tm):
        rows = slice(half * tm, (half + 1) * tm)
        _post_rows(sb_ref[rows, :], fx_ref[rows, :], x_ref[rows, :], p_ref[rows, :], wo_ref, g1_ref,
                   b1_ref, wr_ref, br_ref, wpg_ref, wp_ref, tri_ref, base_ref.at[rows, :],
                   h_ref.at[rows, :], route_ref.at[rows, :], cnt_ref, carry_ref, alpha)


def _post_rows(sb, fx, x, p, wo_ref, g1_ref, b1_ref, wr_ref, br_ref, wpg_ref, wp_ref, tri_ref,
               base_ref, h_ref, route_ref, cnt_ref, carry_ref, alpha):
    mix = _dot(sb, wo_ref[:SB_WIDTH, :]) + _dot(fx, wo_ref[SB_WIDTH:, :])
    h = _layer_norm(alpha * x + mix, g1_ref[...], b1_ref[...])
    h_ref[...] = h

    tm = h.shape[0]
    lane = lax.broadcasted_iota(jnp.int32, (tm, LANES), 1)
    big = jnp.int32(4 * LANES)
    h_hi, h_lo = _split2(h)
    hw = _dot(h_hi, wr_ref[...])
    logits = hw[:, :LANES] + hw[:, LANES:] + _dot(h_lo, wr_ref[:, :LANES]) + br_ref[...]

    def first_argmax(vals):
        vmax = jnp.max(vals, axis=-1, keepdims=True)
        idx = jnp.min(jnp.where(vals == vmax, lane, big), axis=-1, keepdims=True)
        return vmax, idx

    neg = -jnp.inf
    gl = jnp.where(lane < N_GROUPS, logits, neg)
    gmax, gsel = first_argmax(gl)
    g_prob = 1.0 / jnp.sum(jnp.exp(gl - gmax), axis=-1, keepdims=True)
    lo = ROUTE_OFF + gsel * EXPERTS_PER_GROUP
    el = jnp.where((lane >= lo) & (lane < lo + EXPERTS_PER_GROUP), logits, neg)
    v1, i1 = first_argmax(el)
    v2, i2 = first_argmax(jnp.where(lane == i1, neg, el))
    e2 = jnp.exp(v2 - v1)
    gate1 = g_prob / (1.0 + e2)
    gate2 = g_prob * e2 / (1.0 + e2)

    hit1 = lane == i1
    hit2 = lane == i2
    onehot = jnp.where(hit1 | hit2, 1.0, 0.0)
    before = _dot(tri_ref[...], onehot.astype(BF16)) + carry_ref[...]
    rank1 = jnp.sum(jnp.where(hit1, before, 0.0), axis=-1, keepdims=True)
    rank2 = jnp.sum(jnp.where(hit2, before, 0.0), axis=-1, keepdims=True)
    total = before[-1:, :] + onehot[-1:, :]
    carry_ref[...] = total
    cnt_ref[...] = jnp.broadcast_to(total, cnt_ref.shape)

    route = jnp.zeros((tm, LANES), F32)
    fields = ((i1 - ROUTE_OFF).astype(F32), (i2 - ROUTE_OFF).astype(F32), gate1, gate2, rank1, rank2)
    for k, val in enumerate(fields):
        route = jnp.where(lane == k, val, route)
    route_ref[...] = route

    ple = _dot(p.astype(BF16), wp_ref[...])
    gate = jax.nn.sigmoid(_dot(h_hi, wpg_ref[...]))
    base_ref[...] = alpha * h + ple * gate


def _post_call(sb, fx, x2, p2, wo, g1, b1, wr, br, wpg, wp, alpha):
    n, d = x2.shape
    tm = min(POST_TILE, n)
    half = POST_SUB
    row = lax.broadcasted_iota(jnp.int32, (half, half), 0)
    col = lax.broadcasted_iota(jnp.int32, (half, half), 1)
    tri = (col < row).astype(BF16)
    const = lambda i: (0, 0)
    tile = lambda i: (i, 0)
    return pl.pallas_call(
        functools.partial(_post_kernel, alpha=alpha),
        grid=(n // tm,),
        in_specs=[
            pl.BlockSpec((tm, SB_WIDTH), tile),
            pl.BlockSpec((tm, FOX_WIDTH), tile),
            pl.BlockSpec((tm, d), tile),
            pl.BlockSpec((tm, p2.shape[1]), tile),
            pl.BlockSpec(wo.shape, const),
            pl.BlockSpec((1, d), const),
            pl.BlockSpec((1, d), const),
            pl.BlockSpec(wr.shape, const),
            pl.BlockSpec((1, LANES), const),
            pl.BlockSpec(wpg.shape, const),
            pl.BlockSpec(wp.shape, const),
            pl.BlockSpec(tri.shape, const),
        ],
        out_specs=[
            pl.BlockSpec((tm, d), tile),
            pl.BlockSpec((tm, d), tile),
            pl.BlockSpec((tm, LANES), tile),
            pl.BlockSpec((8, LANES), const),
        ],
        out_shape=[
            jax.ShapeDtypeStruct((n, d), F32),
            jax.ShapeDtypeStruct((n, d), F32),
            jax.ShapeDtypeStruct((n, LANES), F32),
            jax.ShapeDtypeStruct((8, LANES), F32),
        ],
        scratch_shapes=[pltpu.VMEM((1, LANES), F32)],
        compiler_params=_cparams("arbitrary"),
        name="post_attn",
    )(sb, fx, x2, p2, wo, g1, b1, wr, br, wpg, wp, tri)


RANK_BITS = 20
ISSUE_UNROLL = 8


def _slot_dest(code_ref, start_ref, idx):
    code = code_ref[idx]
    return start_ref[lax.shift_right_logical(code, RANK_BITS)] + (code & ((1 << RANK_BITS) - 1))


def _dispatch_kernel(code_ref, start_ref, fill_ref, h_ref, xs_ref, zero_ref, sem, fill_sem, *, tt):
    @pl.when(pl.program_id(0) == 0)
    def _():
        zero_ref[...] = jnp.zeros_like(zero_ref)
        tail_start = fill_ref[2 * N_EXPERTS]
        tail_blocks = fill_ref[2 * N_EXPERTS + 1]

        def pad_row(e, r):
            return pltpu.make_async_copy(zero_ref.at[pl.ds(0, 1), :],
                                         xs_ref.at[pl.ds(fill_ref[e] + r, 1), :], fill_sem)

        def tail_block(b):
            rows = pl.ds(pl.multiple_of(tail_start + b * MOE_BLOCK, MOE_BLOCK), MOE_BLOCK)
            return pltpu.make_async_copy(zero_ref, xs_ref.at[rows, :], fill_sem)

        def for_all(wait):
            def per_expert(e, c):
                def per_row(r, c2):
                    cp = pad_row(e, r)
                    cp.wait() if wait else cp.start()
                    return c2
                return lax.fori_loop(0, fill_ref[N_EXPERTS + e], per_row, c)

            def per_block(b, c):
                cp = tail_block(b)
                cp.wait() if wait else cp.start()
                return c

            lax.fori_loop(0, N_EXPERTS, per_expert, 0)
            lax.fori_loop(0, tail_blocks, per_block, 0)

        for_all(False)
        for_all(True)

    def copy(g, u, k):
        dest = _slot_dest(code_ref, start_ref, TOP_K * (g * ISSUE_UNROLL + u) + k)
        return pltpu.make_async_copy(h_ref.at[g, pl.ds(u, 1), :], xs_ref.at[pl.ds(dest, 1), :], sem)

    def start(g, c):
        for u in range(ISSUE_UNROLL):
            for k in range(TOP_K):
                copy(g, u, k).start()
        return c

    def wait(g, c):
        for u in range(ISSUE_UNROLL):
            for k in range(TOP_K):
                copy(g, u, k).wait()
        return c

    lax.fori_loop(0, tt // ISSUE_UNROLL, start, 0)
    lax.fori_loop(0, tt // ISSUE_UNROLL, wait, 0)


def _dispatch_call(h, code_flat, pad_start, fill, cap):
    n, d = h.shape
    tt = min(DISPATCH_TILE, n)
    return pl.pallas_call(
        functools.partial(_dispatch_kernel, tt=tt),
        grid=(n // tt,),
        in_specs=[
            pl.BlockSpec((TOP_K * tt,), lambda i: (i,), memory_space=pltpu.SMEM),
            pl.BlockSpec(memory_space=pltpu.SMEM),
            pl.BlockSpec(memory_space=pltpu.SMEM),
            pl.BlockSpec((tt // ISSUE_UNROLL, ISSUE_UNROLL, d), lambda i: (i, 0, 0)),
        ],
        out_specs=pl.BlockSpec(memory_space=pl.ANY),
        out_shape=jax.ShapeDtypeStruct((cap, d), h.dtype),
        scratch_shapes=[pltpu.VMEM((MOE_BLOCK, d), h.dtype), pltpu.SemaphoreType.DMA(()),
                        pltpu.SemaphoreType.DMA(())],
        compiler_params=_cparams("arbitrary"),
        name="moe_dispatch",
    )(code_flat, pad_start, fill, h.reshape(n // ISSUE_UNROLL, ISSUE_UNROLL, d))


def _expert_kernel(be_ref, nb_ref, xs_ref, wg_ref, wu_ref, wd_ref, ys_ref, wg_bf, wu_bf, wd_bf):
    i = pl.program_id(0)
    used = i < nb_ref[0]

    @pl.when(used & ((i == 0) | (be_ref[i] != be_ref[jnp.maximum(i - 1, 0)])))
    def _():
        wg_bf[...] = wg_ref[0].astype(BF16)
        wu_bf[...] = wu_ref[0].astype(BF16)
        wd_bf[...] = wd_ref[0].astype(BF16)

    @pl.when(used)
    def _():
        xb = xs_ref[...].astype(BF16)
        g = _dot(xb, wg_bf[...])
        u = _dot(xb, wu_bf[...])
        hdn = (g * jax.nn.sigmoid(g)) * u
        ys_ref[...] = _dot(hdn.astype(BF16), wd_bf[...])

    @pl.when(jnp.logical_not(used))
    def _():
        ys_ref[...] = jnp.zeros_like(ys_ref)


def _expert_call(blk_expert, n_used, xs, wg, wu, wd):
    cap, d = xs.shape
    f = wg.shape[2]
    nblk = cap // MOE_BLOCK
    return pl.pallas_call(
        _expert_kernel,
        grid_spec=pltpu.PrefetchScalarGridSpec(
            num_scalar_prefetch=2,
            grid=(nblk,),
            in_specs=[
                pl.BlockSpec((MOE_BLOCK, d), lambda i, be, nb: (i, 0)),
                pl.BlockSpec((1, d, f), lambda i, be, nb: (be[i], 0, 0)),
                pl.BlockSpec((1, d, f), lambda i, be, nb: (be[i], 0, 0)),
                pl.BlockSpec((1, f, d), lambda i, be, nb: (be[i], 0, 0)),
            ],
            out_specs=pl.BlockSpec((MOE_BLOCK, d), lambda i, be, nb: (i, 0)),
            scratch_shapes=[pltpu.VMEM((d, f), BF16), pltpu.VMEM((d, f), BF16),
                            pltpu.VMEM((f, d), BF16)],
        ),
        out_shape=jax.ShapeDtypeStruct((cap, d), F32),
        compiler_params=_cparams("arbitrary"),
        name="moe_experts",
    )(blk_expert, n_used, xs, wg, wu, wd)


def _combine_kernel(code_ref, next_code_ref, start_ref, base_ref, route_ref, g2_ref, b2_ref, ys_ref,
                    o_ref, buf_ref, sems):
    tt = base_ref.shape[0]
    groups = tt // ISSUE_UNROLL
    i = pl.program_id(0)
    last = pl.num_programs(0) - 1

    def copy(codes, buf, g, u, k):
        dest = _slot_dest(codes, start_ref, TOP_K * (g * ISSUE_UNROLL + u) + k)
        return pltpu.make_async_copy(ys_ref.at[pl.ds(dest, 1), :],
                                     buf_ref.at[buf, k * groups + g, pl.ds(u, 1), :], sems.at[buf])

    def for_all_rows(codes, buf, wait):
        def body(g, c):
            for u in range(ISSUE_UNROLL):
                for k in range(TOP_K):
                    cp = copy(codes, buf, g, u, k)
                    if wait:
                        cp.wait()
                    else:
                        cp.start(priority=k)
            return c
        lax.fori_loop(0, groups, body, 0)

    cur = i % 2

    @pl.when(i == 0)
    def _():
        for_all_rows(code_ref, 0, False)

    @pl.when(i < last)
    def _():
        for_all_rows(next_code_ref, 1 - cur, False)

    for_all_rows(code_ref, cur, True)

    route = route_ref[...]
    d = base_ref.shape[1]
    y0 = buf_ref[cur, 0:groups].reshape(tt, d)
    y1 = buf_ref[cur, groups:2 * groups].reshape(tt, d)
    ffn = route[:, 2:3] * y0 + route[:, 3:4] * y1
    o_ref[...] = _layer_norm(base_ref[...] + ffn, g2_ref[...], b2_ref[...])


def _combine_call(code_flat, pad_start, base, route, g2, b2, ys):
    n, d = base.shape
    tt = min(MOVE_TILE, n)
    steps = n // tt
    return pl.pallas_call(
        _combine_kernel,
        grid=(steps,),
        in_specs=[
            pl.BlockSpec((TOP_K * tt,), lambda i: (i,), memory_space=pltpu.SMEM),
            pl.BlockSpec((TOP_K * tt,), lambda i: (jnp.minimum(i + 1, steps - 1),),
                         memory_space=pltpu.SMEM),
            pl.BlockSpec(memory_space=pltpu.SMEM),
            pl.BlockSpec((tt, d), lambda i: (i, 0)),
            pl.BlockSpec((tt, LANES), lambda i: (i, 0)),
            pl.BlockSpec((1, d), lambda i: (0, 0)),
            pl.BlockSpec((1, d), lambda i: (0, 0)),
            pl.BlockSpec(memory_space=pl.ANY),
        ],
        out_specs=pl.BlockSpec((tt, d), lambda i: (i, 0)),
        out_shape=jax.ShapeDtypeStruct((n, d), F32),
        scratch_shapes=[pltpu.VMEM((2, TOP_K * tt // ISSUE_UNROLL, ISSUE_UNROLL, d), F32),
                        pltpu.SemaphoreType.DMA((2,))],
        compiler_params=_cparams("arbitrary"),
        name="moe_combine",
    )(code_flat, code_flat, pad_start, base, route, g2, b2, ys)


def _layer(h, p_i, w_in, b_forget, w_out, g1, b1, w_group, b_group, w_router, b_router,
           w_gate, w_up, w_down, w_ple, w_ple_gate, g2, b2, alpha):
    bsz, seq, d = h.shape
    n = bsz * seq
    x2 = h.reshape(n, d)

    wr = jnp.pad(jnp.concatenate([w_group, w_router], axis=1).astype(F32),
                 ((0, 0), (0, LANES - N_GROUPS - N_EXPERTS)))
    wr_hi = wr.astype(BF16)
    wr = jnp.concatenate([wr_hi, (wr - wr_hi.astype(F32)).astype(BF16)], axis=1)
    br = jnp.pad(jnp.concatenate([b_group, b_router]).astype(F32),
                 (0, LANES - N_GROUPS - N_EXPERTS)).reshape(1, LANES)

    qsb, qfx, kaug, vt, cfmin = _proj_call(x2, w_in, b_forget, seq)
    per_tile = min(ROW_TILE, seq) // ATT_BLOCK
    cfmin = cfmin[:, :per_tile, :FOX_HEADS].reshape(bsz, seq // ATT_BLOCK, FOX_HEADS)
    cfmin = cfmin.transpose(0, 2, 1).reshape(-1)

    sb = _sb_call(qsb.reshape(bsz, seq, SB_COLS)).reshape(n, SB_WIDTH)
    fx = _fox_call(qfx.reshape(bsz, seq, FOX_WIDTH), kaug.reshape(bsz, seq, FOX_HEADS * LANES),
                   vt, cfmin).reshape(n, FOX_WIDTH)

    base, h1, route, counts = _post_call(
        sb, fx, x2, p_i.reshape(n, -1), w_out.astype(BF16), g1.reshape(1, d), b1.reshape(1, d),
        wr, br, w_ple_gate.astype(BF16), w_ple.astype(BF16), alpha)

    counts = counts[0, ROUTE_OFF:ROUTE_OFF + N_EXPERTS].astype(jnp.int32)
    padded = ((counts + MOE_BLOCK - 1) // MOE_BLOCK) * MOE_BLOCK
    pad_end = jnp.cumsum(padded)
    pad_start = pad_end - padded
    eid = route[:, 0:TOP_K].astype(jnp.int32)
    rank = route[:, 4:4 + TOP_K].astype(jnp.int32)
    code = (eid * (1 << RANK_BITS) + rank).reshape(n * TOP_K)
    cap = n * TOP_K + N_EXPERTS * MOE_BLOCK
    nblk = cap // MOE_BLOCK
    blk_first = jnp.arange(nblk, dtype=jnp.int32)[:, None] * MOE_BLOCK
    blk_expert = jnp.minimum(jnp.sum((pad_end[None, :] <= blk_first).astype(jnp.int32), axis=1),
                             N_EXPERTS - 1)
    n_used = (pad_end[-1:] // MOE_BLOCK).astype(jnp.int32)
    fill = jnp.concatenate([pad_start + counts, padded - counts, pad_end[-1:],
                            (cap - pad_end[-1:]) // MOE_BLOCK]).astype(jnp.int32)

    xs = _dispatch_call(h1, code, pad_start, fill, cap)
    ys = _expert_call(blk_expert, n_used, xs, w_gate, w_up, w_down)
    out = _combine_call(code, pad_start, base, route, g2.reshape(1, d), b2.reshape(1, d), ys)
    return out.reshape(bsz, seq, d)


def kernel(x, p, w_in, b_forget, w_out, ln_mix_g, ln_mix_b, w_group, b_group, w_router, b_router,
           w_gate, w_up, w_down, w_ple, w_ple_gate, ln_ffn_g, ln_ffn_b):
    depth = w_in.shape[0]
    alpha = (2 * depth) ** 0.25
    h = x
    for i in range(depth):
        h = _layer(h, p[i], w_in[i], b_forget[i], w_out[i], ln_mix_g[i], ln_mix_b[i], w_group[i],
                   b_group[i], w_router[i], b_router[i], w_gate[i], w_up[i], w_down[i], w_ple[i],
                   w_ple_gate[i], ln_ffn_g[i], ln_ffn_b[i], alpha)
    return h
```

```python
import functools

import jax
import jax.numpy as jnp
from jax import lax
from jax.experimental import pallas as pl
from jax.experimental.pallas import tpu as pltpu

F32 = jnp.float32
BF16 = jnp.bfloat16

HEAD_DIM = 64
SB_HEADS = 8
FOX_HEADS = 8
SB_WIDTH = SB_HEADS * HEAD_DIM
FOX_WIDTH = FOX_HEADS * HEAD_DIM
N_GROUPS = 4
EXPERTS_PER_GROUP = 8
N_EXPERTS = N_GROUPS * EXPERTS_PER_GROUP
TOP_K = 2
LN_EPS = 1e-5
SCALE = HEAD_DIM ** -0.5

LANES = 128
ATT_BLOCK = 256
ROW_TILE = 512
POST_TILE = 512
POST_SUB = 256
MOE_BLOCK = 256
MOVE_TILE = 256
DISPATCH_TILE = 512
ROUTE_OFF = N_GROUPS
VMEM_LIMIT = 56 * 1024 * 1024


def _cparams(*sem):
    return pltpu.CompilerParams(dimension_semantics=sem, vmem_limit_bytes=VMEM_LIMIT)


def _dot(a, b):
    return jnp.dot(a, b, preferred_element_type=F32)


def _dot_nt(a, b):
    return lax.dot_general(a, b, (((1,), (1,)), ((), ())), preferred_element_type=F32)


def _split2(x):
    hi = x.astype(BF16)
    lo = (x - hi.astype(F32)).astype(BF16)
    return hi, lo


def _split3(x):
    hi = x.astype(BF16)
    r = x - hi.astype(F32)
    mid = r.astype(BF16)
    lo = (r - mid.astype(F32)).astype(BF16)
    return hi, mid, lo


def _layer_norm(u, g, b):
    mu = jnp.mean(u, axis=-1, keepdims=True)
    d = u - mu
    var = jnp.mean(d * d, axis=-1, keepdims=True)
    return d * lax.rsqrt(var + LN_EPS) * g + b


SB_COLS = 3 * SB_WIDTH
FXK_OFF = SB_COLS + FOX_WIDTH
FGATE_OFF = FXK_OFF + FOX_HEADS * LANES
PROJ_COLS = FGATE_OFF + LANES
AUG_LANE = HEAD_DIM
AUG_TERMS = 3


def _proj_kernel(x_ref, w_ref, wvt_ref, bf_ref, tri_ref, sel_ref, qsb_ref, qfx_ref, kaug_ref,
                 vt_ref, cfmin_ref, carry_ref, *, tiles_per_seq):
    i = pl.program_id(0)
    t = ATT_BLOCK

    @pl.when(i % tiles_per_seq == 0)
    def _():
        carry_ref[...] = jnp.zeros_like(carry_ref)

    xb = x_ref[...].astype(BF16)
    tm = xb.shape[0]
    chunk = 512
    for c in range(SB_COLS // chunk):
        sl = slice(c * chunk, (c + 1) * chunk)
        res = _dot(xb, w_ref[:, sl])
        if SB_WIDTH <= c * chunk < 2 * SB_WIDTH:
            res = res * LOG2E
        qsb_ref[:, sl] = res.astype(BF16)
    qfx_ref[...] = _dot(xb, w_ref[:, SB_COLS:FXK_OFF]).astype(BF16)

    fl = _dot(xb, w_ref[:, FGATE_OFF:]) + bf_ref[...]
    lf = jnp.minimum(fl, 0.0) - jnp.log(1.0 + jnp.exp(-jnp.abs(fl)))
    tri = tri_ref[...]
    hi, mid, lo = _split3(lf)
    cum = _dot(tri, hi) + _dot(tri, mid) + _dot(tri, lo) + carry_ref[...]
    carry_ref[...] = cum[-1:, :]

    cum2 = cum * LOG2E
    nhi, nmid, nlo = _split3(-cum2)
    lane = lax.broadcasted_iota(jnp.int32, (tm, LANES), 1)
    terms = jnp.where(lane < FOX_HEADS, nhi, jnp.where(lane < 2 * FOX_HEADS, nmid, nlo))
    aug = _dot(terms, sel_ref[...])
    for c in range((FGATE_OFF - FXK_OFF) // chunk):
        sl = slice(c * chunk, (c + 1) * chunk)
        wsl = slice(FXK_OFF + c * chunk, FXK_OFF + (c + 1) * chunk)
        kaug_ref[:, sl] = (_dot(xb, w_ref[:, wsl]) * LOG2E + aug[:, sl]).astype(BF16)

    mins = []
    for c in range(tm // t):
        rows = slice(c * t, (c + 1) * t)
        vt_ref[c] = _dot_nt(wvt_ref[...], xb[rows, :]).astype(BF16)
        mins.append(jnp.min(cum2[rows, :], axis=0, keepdims=True))
    mins.append(jnp.zeros((8 - tm // t, LANES), F32))
    cfmin_ref[0] = jnp.concatenate(mins, axis=0)


def _proj_weights(w_in, b_forget):
    d = w_in.shape[0]
    off_q_fx = SB_COLS
    off_k_fx = off_q_fx + FOX_WIDTH
    off_v_fx = off_k_fx + FOX_WIDTH
    off_f = off_v_fx + FOX_WIDTH
    wk = w_in[:, off_k_fx:off_v_fx].reshape(d, FOX_HEADS, HEAD_DIM)
    wk = jnp.pad(wk, ((0, 0), (0, 0), (0, LANES - HEAD_DIM))).reshape(d, FOX_HEADS * LANES)
    wf = jnp.tile(w_in[:, off_f:off_f + FOX_HEADS], (1, AUG_TERMS))
    wf = jnp.pad(wf, ((0, 0), (0, LANES - AUG_TERMS * FOX_HEADS)))
    w_pad = jnp.concatenate([w_in[:, :off_k_fx], wk, wf], axis=1).astype(BF16)
    wvt = w_in[:, off_v_fx:off_f].T.astype(BF16)
    bf = jnp.pad(jnp.tile(b_forget.astype(F32), AUG_TERMS), (0, LANES - AUG_TERMS * FOX_HEADS))
    src = lax.broadcasted_iota(jnp.int32, (LANES, FOX_HEADS * LANES), 0)
    dst = lax.broadcasted_iota(jnp.int32, (LANES, FOX_HEADS * LANES), 1)
    head, term = src % FOX_HEADS, src // FOX_HEADS
    sel = ((term < AUG_TERMS) & (dst == head * LANES + AUG_LANE + term)).astype(BF16)
    return w_pad, wvt, bf.reshape(1, LANES), sel


def _proj_call(x2, w_in, b_forget, seq):
    n, d = x2.shape
    tm = min(ROW_TILE, seq)
    t = ATT_BLOCK
    w_pad, wvt, bf_pad, sel = _proj_weights(w_in, b_forget)
    row = lax.broadcasted_iota(jnp.int32, (tm, tm), 0)
    col = lax.broadcasted_iota(jnp.int32, (tm, tm), 1)
    tri = (col <= row).astype(BF16)
    const = lambda i: (0, 0)
    tile = lambda i: (i, 0)
    return pl.pallas_call(
        functools.partial(_proj_kernel, tiles_per_seq=seq // tm),
        grid=(n // tm,),
        in_specs=[
            pl.BlockSpec((tm, d), tile),
            pl.BlockSpec((d, PROJ_COLS), const),
            pl.BlockSpec((FOX_WIDTH, d), const),
            pl.BlockSpec((1, LANES), const),
            pl.BlockSpec((tm, tm), const),
            pl.BlockSpec(sel.shape, const),
        ],
        out_specs=[
            pl.BlockSpec((tm, SB_COLS), tile),
            pl.BlockSpec((tm, FOX_WIDTH), tile),
            pl.BlockSpec((tm, FOX_HEADS * LANES), tile),
            pl.BlockSpec((tm // t, FOX_WIDTH, t), lambda i: (i, 0, 0)),
            pl.BlockSpec((1, 8, LANES), lambda i: (i, 0, 0)),
        ],
        out_shape=[
            jax.ShapeDtypeStruct((n, SB_COLS), BF16),
            jax.ShapeDtypeStruct((n, FOX_WIDTH), BF16),
            jax.ShapeDtypeStruct((n, FOX_HEADS * LANES), BF16),
            jax.ShapeDtypeStruct((n // t, FOX_WIDTH, t), BF16),
            jax.ShapeDtypeStruct((n // tm, 8, LANES), F32),
        ],
        scratch_shapes=[pltpu.VMEM((1, LANES), F32)],
        compiler_params=_cparams("arbitrary"),
        name="proj",
    )(x2, w_pad, wvt, bf_pad, tri, sel)


ZERO_EXP2 = 152.0
LOG2E = 1.4426950408889634


def _softplus2(z2):
    return jnp.maximum(z2, 0.0) + jnp.log2(1.0 + jnp.exp2(-jnp.abs(z2)))


def _key_absmax(k_ref, kam_ref):
    seq, width = k_ref.shape[1], k_ref.shape[2]
    rows = min(512, seq)

    def body(i, m):
        blk = k_ref[0, pl.ds(pl.multiple_of(i * rows, rows), rows), :]
        return jnp.maximum(m, jnp.max(jnp.abs(blk.astype(F32)), axis=0, keepdims=True))

    kam = lax.fori_loop(0, seq // rows, body, jnp.zeros((1, width), F32))
    kam_ref[...] = jnp.broadcast_to(kam, kam_ref.shape)


def _key_sqnorm_max(k_ref, out_ref):
    seq, width = k_ref.shape[1], k_ref.shape[2]
    rows = min(512, seq)
    groups = width // LANES
    lane = lax.broadcasted_iota(jnp.int32, (rows, LANES), 1)

    def body(i, m):
        new = []
        for g in range(groups):
            blk = k_ref[0, pl.ds(pl.multiple_of(i * rows, rows), rows), g * LANES:(g + 1) * LANES]
            sq = jnp.where(lane < HEAD_DIM, jnp.square(blk.astype(F32)), 0.0)
            norms = jnp.sum(sq, axis=-1, keepdims=True)
            new.append(jnp.maximum(m[g], jnp.max(norms, axis=0, keepdims=True)))
        return tuple(new)

    best = lax.fori_loop(0, seq // rows, body, (jnp.zeros((1, 1), F32),) * groups)
    for g in range(groups):
        out_ref[:, g * LANES:(g + 1) * LANES] = jnp.broadcast_to(best[g], (out_ref.shape[0], LANES))


def _score_bound(qh, kam):
    zmax = jnp.sum(jnp.abs(qh.astype(F32)) * kam, axis=-1, keepdims=True)
    return zmax * (1.0 + 2.0 ** -10) + 2.0 ** -10


SB_QBLOCKS = 2


def _sb_kernel(q_ref, k_ref, v_ref, tri_ref, o_ref, kam_ref):
    t = ATT_BLOCK
    step = pl.program_id(2)

    @pl.when(step == 0)
    def _():
        _key_absmax(k_ref, kam_ref)

    lane = lax.broadcasted_iota(jnp.int32, (t, LANES), 1)
    row = lax.broadcasted_iota(jnp.int32, (t, t), 0)
    col = lax.broadcasted_iota(jnp.int32, (t, t), 1)
    earlier = col < row
    tri = tri_ref[...]
    kam = kam_ref[0:1, :]

    def block(qs, kb, state, diag, valid=None):
        start = pl.multiple_of(kb * t, t)
        kblk = k_ref[0, pl.ds(start, t), :]
        vblk = v_ref[0, pl.ds(start, t), :]
        new = []
        for h in range(2):
            carry, acc = state[2 * h], state[2 * h + 1]
            z = _dot_nt(qs[h], kblk)
            sp = _softplus2(z)
            if diag:
                sp = jnp.where(earlier, sp, 0.0)
            if valid is not None:
                sp = jnp.where(valid, sp, 0.0)
            hi, lo = _split2(sp)
            cum = _dot(hi, tri) + _dot(lo, tri)
            w = jnp.exp2(z - cum - carry)
            if diag:
                w = jnp.where(earlier, w, 0.0)
            if valid is not None:
                w = jnp.where(valid, w, 0.0)
            acc = acc + _dot(w.astype(BF16), vblk)
            carry = carry + cum[:, 0:1]
            new += [carry, acc]
        return tuple(new)

    init = (jnp.zeros((t, 1), F32), jnp.zeros((t, LANES), F32)) * 2
    streams = []
    for x in range(SB_QBLOCKS):
        qi = step * SB_QBLOCKS + x
        q = q_ref[0, x * t:(x + 1) * t, :] * jnp.asarray(SCALE, BF16)
        zero_q = jnp.zeros_like(q)
        qs = (jnp.where(lane < HEAD_DIM, q, zero_q), jnp.where(lane >= HEAD_DIM, q, zero_q))
        zmax = tuple(_score_bound(qh, kam) for qh in qs)
        state = block(qs, qi, init, True)
        state = block(qs, jnp.maximum(qi - 1, 0), state, False, valid=(qi >= 1) if x == 0 else None)
        streams.append((qi, qs, zmax, state))

    for x, (qi, qs, zmax, state) in enumerate(streams):
        def live(state, zmax=zmax):
            worst = jnp.maximum(jnp.max(zmax[0] - state[0]), jnp.max(zmax[1] - state[2]))
            return worst > -ZERO_EXP2

        def cond(c):
            return (c[0] >= 0) & c[1]

        def body(c, qs=qs, live=live):
            state = block(qs, c[0], c[2:], False)
            return (c[0] - 1, live(state)) + state

        c = lax.while_loop(cond, body, (qi - 2, live(state)) + state)
        o_ref[0, x * t:(x + 1) * t, :] = jnp.where(lane < HEAD_DIM, c[3], c[5]).astype(o_ref.dtype)


def _att_tri():
    t = ATT_BLOCK
    s = lax.broadcasted_iota(jnp.int32, (t, t), 0)
    j = lax.broadcasted_iota(jnp.int32, (t, t), 1)
    return (s >= j).astype(BF16)


def _sb_call(qkv):
    b, seq, _ = qkv.shape
    t = ATT_BLOCK
    pairs = SB_WIDTH // LANES
    q_off, k_off, v_off = 0, SB_WIDTH // LANES, 2 * SB_WIDTH // LANES
    rows = SB_QBLOCKS * t
    return pl.pallas_call(
        _sb_kernel,
        grid=(b, pairs, seq // rows),
        in_specs=[
            pl.BlockSpec((1, rows, LANES), lambda bi, hp, qi: (bi, qi, q_off + hp)),
            pl.BlockSpec((1, seq, LANES), lambda bi, hp, qi: (bi, 0, k_off + hp)),
            pl.BlockSpec((1, seq, LANES), lambda bi, hp, qi: (bi, 0, v_off + hp)),
            pl.BlockSpec((t, t), lambda bi, hp, qi: (0, 0)),
        ],
        out_specs=pl.BlockSpec((1, rows, LANES), lambda bi, hp, qi: (bi, qi, hp)),
        out_shape=jax.ShapeDtypeStruct((b, seq, SB_WIDTH), BF16),
        scratch_shapes=[pltpu.VMEM((8, LANES), F32)],
        compiler_params=_cparams("arbitrary", "arbitrary", "arbitrary"),
        name="sb_attn",
    )(qkv, qkv, qkv, _att_tri())


FOX_QBLOCKS = 2


def _fox_kernel(cfmin_ref, q_ref, k_ref, vt_ref, o_ref, s_a, s_b, p_a, p_b, kam_ref, *, nblk):
    t = ATT_BLOCK
    bi, hp, step = pl.program_id(0), pl.program_id(1), pl.program_id(2)

    @pl.when(step == 0)
    def _():
        _key_sqnorm_max(k_ref, kam_ref)

    lane = lax.broadcasted_iota(jnp.int32, (t, LANES), 1)
    feat = lax.broadcasted_iota(jnp.int32, (LANES, t), 0)
    key = lax.broadcasted_iota(jnp.int32, (t, t), 0)
    qry = lax.broadcasted_iota(jnp.int32, (t, t), 1)
    visible = key <= qry
    ones_aug = jnp.where(lane < AUG_LANE + AUG_TERMS, 1.0, 0.0)
    own_rows = (feat < HEAD_DIM, feat >= HEAD_DIM)
    neg_inf = jnp.full((t, t), -jnp.inf, F32)

    chains = [(x, h) for x in range(FOX_QBLOCKS) for h in range(2)]
    qis = [step * FOX_QBLOCKS + x for x in range(FOX_QBLOCKS)]
    qt, zmax = [], []
    for x, h in chains:
        qf = q_ref[0, x * t:(x + 1) * t, :].astype(F32) * SCALE
        qh = qf if h == 0 else pltpu.roll(qf, HEAD_DIM, 1)
        qht = jnp.where(lane < HEAD_DIM, qh, ones_aug).T
        qt.append(qht.astype(BF16))
        qn2 = jnp.sum(jnp.where(feat < HEAD_DIM, qht * qht, 0.0), axis=0, keepdims=True)
        kn2 = kam_ref[0:1, h * LANES:h * LANES + 1]
        zmax.append(jnp.sqrt(qn2 * kn2) * (1.0 + 2.0 ** -7) + 2.0 ** -7)

    def qk(c, off):
        h = chains[c][1]
        kb = jnp.maximum(qis[chains[c][0]] - off, 0)
        start = pl.multiple_of(kb * t, t)
        return _dot(k_ref[0, pl.ds(start, t), h * LANES:(h + 1) * LANES], qt[c])

    def softmax_part(s, m):
        m_new = jnp.maximum(m, jnp.max(s, axis=0, keepdims=True))
        return m_new, jnp.exp2(s - m_new).astype(BF16), jnp.exp2(m - m_new)

    def pv(c, off, alpha, acc, p_buf):
        vt = vt_ref[jnp.maximum(qis[chains[c][0]] - off, 0)]
        vmod = jnp.where(own_rows[chains[c][1]], vt, jnp.ones_like(vt))
        return alpha * acc + _dot(vmod, p_buf[c])

    def live(off, ms):
        alive = None
        for c, (x, h) in enumerate(chains):
            kb = qis[x] - off
            cf_min = cfmin_ref[(bi * FOX_HEADS + 2 * hp + h) * nblk + jnp.maximum(kb, 0)]
            a_c = (kb >= 0) & (jnp.max(zmax[c] - ms[c]) - cf_min > -ZERO_EXP2)
            alive = a_c if alive is None else alive | a_c
        return alive

    def scores(c, off, raw):
        return jnp.where(qis[chains[c][0]] - off >= 0, raw, neg_inf)

    n = len(chains)
    s_bufs, p_bufs = (s_a, s_b), (p_a, p_b)
    ms, alphas = [], []
    for c in range(n):
        m, p, alpha = softmax_part(jnp.where(visible, qk(c, 0), neg_inf), jnp.full((1, t), -jnp.inf, F32))
        p_a[c] = p
        s_b[c] = qk(c, 1)
        ms.append(m)
        alphas.append(alpha)
    accs = [jnp.zeros((LANES, t), F32)] * n

    def cond(st):
        return (st[0] <= qis[-1]) & st[1]

    def body(st):
        j = st[0]
        ms, alphas, accs = list(st[2:2 + n]), list(st[2 + n:2 + 2 * n]), list(st[2 + 2 * n:])
        for half in range(2):
            off = j + half
            p_old, p_new = p_bufs[half], p_bufs[1 - half]
            s_cur, s_nxt = s_bufs[1 - half], s_bufs[half]
            for c in range(n):
                accs[c] = pv(c, off - 1, alphas[c], accs[c], p_old)
            for c in range(n):
                certain = half == 0 and chains[c][0] == FOX_QBLOCKS - 1
                s = s_cur[c] if certain else scores(c, off, s_cur[c])
                ms[c], p, alphas[c] = softmax_part(s, ms[c])
                p_new[c] = p
            for c in range(n):
                s_nxt[c] = qk(c, off + 1)
        return (j + 2, live(j + 2, ms)) + tuple(ms) + tuple(alphas) + tuple(accs)

    st = lax.while_loop(cond, body, (jnp.int32(1), live(1, ms)) + tuple(ms) + tuple(alphas) + tuple(accs))
    j = st[0]
    accs = [pv(c, j - 1, st[2 + n + c], st[2 + 2 * n + c], p_a) for c in range(n)]
    for x in range(FOX_QBLOCKS):
        acc0, acc1 = accs[2 * x], accs[2 * x + 1]
        out_t = jnp.concatenate([acc0[:HEAD_DIM] / acc0[HEAD_DIM:HEAD_DIM + 1],
                                 acc1[HEAD_DIM:] / acc1[0:1]], axis=0)
        o_ref[0, x * t:(x + 1) * t, :] = out_t.T.astype(o_ref.dtype)


def _fox_call(qfx, kaug, vt, cfmin):
    b, seq, _ = qfx.shape
    t = ATT_BLOCK
    nblk = seq // t
    rows = FOX_QBLOCKS * t
    chains = 2 * FOX_QBLOCKS
    pairs = FOX_WIDTH // LANES
    return pl.pallas_call(
        functools.partial(_fox_kernel, nblk=nblk),
        grid=(b, pairs, seq // rows),
        in_specs=[
            pl.BlockSpec(memory_space=pltpu.SMEM),
            pl.BlockSpec((1, rows, LANES), lambda bi, hp, qi: (bi, qi, hp)),
            pl.BlockSpec((1, seq, 2 * LANES), lambda bi, hp, qi: (bi, 0, hp)),
            pl.BlockSpec((nblk, LANES, t), lambda bi, hp, qi: (bi, hp, 0)),
        ],
        out_specs=pl.BlockSpec((1, rows, LANES), lambda bi, hp, qi: (bi, qi, hp)),
        out_shape=jax.ShapeDtypeStruct((b, seq, FOX_WIDTH), BF16),
        scratch_shapes=[pltpu.VMEM((chains, t, t), F32), pltpu.VMEM((chains, t, t), F32),
                        pltpu.VMEM((chains, t, t), BF16), pltpu.VMEM((chains, t, t), BF16),
                        pltpu.VMEM((8, 2 * LANES), F32)],
        compiler_params=_cparams("arbitrary", "arbitrary", "arbitrary"),
        name="fox_attn",
    )(cfmin, qfx, kaug, vt)


def _post_kernel(sb_ref, fx_ref, x_ref, p_ref, wo_ref, g1_ref, b1_ref, wr_ref, br_ref, wpg_ref,
                 wp_ref, tri_ref, base_ref, h_ref, route_ref, cnt_ref, carry_ref, *, alpha):
    i = pl.program_id(0)

    @pl.when(i == 0)
    def _():
        carry_ref[...] = jnp.zeros_like(carry_ref)

    tm = POST_SUB
    for half in range(x_ref.shape[0] // tm):
        rows = slice(half * tm, (half + 1) * tm)
        _post_rows(sb_ref[rows, :], fx_ref[rows, :], x_ref[rows, :], p_ref[rows, :], wo_ref, g1_ref,
                   b1_ref, wr_ref, br_ref, wpg_ref, wp_ref, tri_ref, base_ref.at[rows, :],
                   h_ref.at[rows, :], route_ref.at[rows, :], cnt_ref, carry_ref, alpha)


def _post_rows(sb, fx, x, p, wo_ref, g1_ref, b1_ref, wr_ref, br_ref, wpg_ref, wp_ref, tri_ref,
               base_ref, h_ref, route_ref, cnt_ref, carry_ref, alpha):
    mix = _dot(sb, wo_ref[:SB_WIDTH, :]) + _dot(fx, wo_ref[SB_WIDTH:, :])
    h = _layer_norm(alpha * x + mix, g1_ref[...], b1_ref[...])
    h_ref[...] = h

    tm = h.shape[0]
    lane = lax.broadcasted_iota(jnp.int32, (tm, LANES), 1)
    big = jnp.int32(4 * LANES)
    h_hi, h_lo = _split2(h)
    hw = _dot(h_hi, wr_ref[...])
    logits = hw[:, :LANES] + hw[:, LANES:] + _dot(h_lo, wr_ref[:, :LANES]) + br_ref[...]

    def first_argmax(vals):
        vmax = jnp.max(vals, axis=-1, keepdims=True)
        idx = jnp.min(jnp.where(vals == vmax, lane, big), axis=-1, keepdims=True)
        return vmax, idx

    neg = -jnp.inf
    gl = jnp.where(lane < N_GROUPS, logits, neg)
    gmax, gsel = first_argmax(gl)
    g_prob = 1.0 / jnp.sum(jnp.exp(gl - gmax), axis=-1, keepdims=True)
    lo = ROUTE_OFF + gsel * EXPERTS_PER_GROUP
    el = jnp.where((lane >= lo) & (lane < lo + EXPERTS_PER_GROUP), logits, neg)
    v1, i1 = first_argmax(el)
    v2, i2 = first_argmax(jnp.where(lane == i1, neg, el))
    e2 = jnp.exp(v2 - v1)
    gate1 = g_prob / (1.0 + e2)
    gate2 = g_prob * e2 / (1.0 + e2)

    hit1 = lane == i1
    hit2 = lane == i2
    onehot = jnp.where(hit1 | hit2, 1.0, 0.0)
    before = _dot(tri_ref[...], onehot.astype(BF16)) + carry_ref[...]
    rank1 = jnp.sum(jnp.where(hit1, before, 0.0), axis=-1, keepdims=True)
    rank2 = jnp.sum(jnp.where(hit2, before, 0.0), axis=-1, keepdims=True)
    total = before[-1:, :] + onehot[-1:, :]
    carry_ref[...] = total
    cnt_ref[...] = jnp.broadcast_to(total, cnt_ref.shape)

    route = jnp.zeros((tm, LANES), F32)
    fields = ((i1 - ROUTE_OFF).astype(F32), (i2 - ROUTE_OFF).astype(F32), gate1, gate2, rank1, rank2)
    for k, val in enumerate(fields):
        route = jnp.where(lane == k, val, route)
    route_ref[...] = route

    ple = _dot(p.astype(BF16), wp_ref[...])
    gate = jax.nn.sigmoid(_dot(h_hi, wpg_ref[...]))
    base_ref[...] = alpha * h + ple * gate


def _post_call(sb, fx, x2, p2, wo, g1, b1, wr, br, wpg, wp, alpha):
    n, d = x2.shape
    tm = min(POST_TILE, n)
    half = POST_SUB
    row = lax.broadcasted_iota(jnp.int32, (half, half), 0)
    col = lax.broadcasted_iota(jnp.int32, (half, half), 1)
    tri = (col < row).astype(BF16)
    const = lambda i: (0, 0)
    tile = lambda i: (i, 0)
    return pl.pallas_call(
        functools.partial(_post_kernel, alpha=alpha),
        grid=(n // tm,),
        in_specs=[
            pl.BlockSpec((tm, SB_WIDTH), tile),
            pl.BlockSpec((tm, FOX_WIDTH), tile),
            pl.BlockSpec((tm, d), tile),
            pl.BlockSpec((tm, p2.shape[1]), tile),
            pl.BlockSpec(wo.shape, const),
            pl.BlockSpec((1, d), const),
            pl.BlockSpec((1, d), const),
            pl.BlockSpec(wr.shape, const),
            pl.BlockSpec((1, LANES), const),
            pl.BlockSpec(wpg.shape, const),
            pl.BlockSpec(wp.shape, const),
            pl.BlockSpec(tri.shape, const),
        ],
        out_specs=[
            pl.BlockSpec((tm, d), tile),
            pl.BlockSpec((tm, d), tile),
            pl.BlockSpec((tm, LANES), tile),
            pl.BlockSpec((8, LANES), const),
        ],
        out_shape=[
            jax.ShapeDtypeStruct((n, d), F32),
            jax.ShapeDtypeStruct((n, d), F32),
            jax.ShapeDtypeStruct((n, LANES), F32),
            jax.ShapeDtypeStruct((8, LANES), F32),
        ],
        scratch_shapes=[pltpu.VMEM((1, LANES), F32)],
        compiler_params=_cparams("arbitrary"),
        name="post_attn",
    )(sb, fx, x2, p2, wo, g1, b1, wr, br, wpg, wp, tri)


RANK_BITS = 20
ISSUE_UNROLL = 8


def _slot_dest(code_ref, start_ref, idx):
    code = code_ref[idx]
    return start_ref[lax.shift_right_logical(code, RANK_BITS)] + (code & ((1 << RANK_BITS) - 1))


def _dispatch_kernel(code_ref, start_ref, fill_ref, h_ref, xs_ref, zero_ref, sem, fill_sem, *, tt):
    @pl.when(pl.program_id(0) == 0)
    def _():
        zero_ref[...] = jnp.zeros_like(zero_ref)
        tail_start = fill_ref[2 * N_EXPERTS]
        tail_blocks = fill_ref[2 * N_EXPERTS + 1]

        def pad_row(e, r):
            return pltpu.make_async_copy(zero_ref.at[pl.ds(0, 1), :],
                                         xs_ref.at[pl.ds(fill_ref[e] + r, 1), :], fill_sem)

        def pad_group(first, g):
            rows = pl.ds(pl.multiple_of(first + g * ISSUE_UNROLL, ISSUE_UNROLL), ISSUE_UNROLL)
            return pltpu.make_async_copy(zero_ref.at[pl.ds(0, ISSUE_UNROLL), :], xs_ref.at[rows, :],
                                         fill_sem)

        def tail_block(b):
            rows = pl.ds(pl.multiple_of(tail_start + b * MOE_BLOCK, MOE_BLOCK), MOE_BLOCK)
            return pltpu.make_async_copy(zero_ref, xs_ref.at[rows, :], fill_sem)

        def for_all(wait):
            def per_expert(e, c):
                count = fill_ref[N_EXPERTS + e]
                singles = count & (ISSUE_UNROLL - 1)

                def per_row(r, c2):
                    cp = pad_row(e, r)
                    cp.wait() if wait else cp.start()
                    return c2

                def per_group(g, c2):
                    cp = pad_group(fill_ref[e] + singles, g)
                    cp.wait() if wait else cp.start()
                    return c2

                lax.fori_loop(0, singles, per_row, c)
                return lax.fori_loop(0, count // ISSUE_UNROLL, per_group, c)

            def per_block(b, c):
                cp = tail_block(b)
                cp.wait() if wait else cp.start()
                return c

            lax.fori_loop(0, N_EXPERTS, per_expert, 0)
            lax.fori_loop(0, tail_blocks, per_block, 0)

        for_all(False)
        for_all(True)

    def copy(g, u, k):
        dest = _slot_dest(code_ref, start_ref, TOP_K * (g * ISSUE_UNROLL + u) + k)
        return pltpu.make_async_copy(h_ref.at[g, pl.ds(u, 1), :], xs_ref.at[pl.ds(dest, 1), :], sem)

    def start(g, c):
        for u in range(ISSUE_UNROLL):
            for k in range(TOP_K):
                copy(g, u, k).start()
        return c

    def wait(g, c):
        for u in range(ISSUE_UNROLL):
            for k in range(TOP_K):
                copy(g, u, k).wait()
        return c

    lax.fori_loop(0, tt // ISSUE_UNROLL, start, 0)
    lax.fori_loop(0, tt // ISSUE_UNROLL, wait, 0)


def _dispatch_call(h, code_flat, pad_start, fill, cap):
    n, d = h.shape
    tt = min(DISPATCH_TILE, n)
    return pl.pallas_call(
        functools.partial(_dispatch_kernel, tt=tt),
        grid=(n // tt,),
        in_specs=[
            pl.BlockSpec((TOP_K * tt,), lambda i: (i,), memory_space=pltpu.SMEM),
            pl.BlockSpec(memory_space=pltpu.SMEM),
            pl.BlockSpec(memory_space=pltpu.SMEM),
            pl.BlockSpec((tt // ISSUE_UNROLL, ISSUE_UNROLL, d), lambda i: (i, 0, 0)),
        ],
        out_specs=pl.BlockSpec(memory_space=pl.ANY),
        out_shape=jax.ShapeDtypeStruct((cap, d), h.dtype),
        scratch_shapes=[pltpu.VMEM((MOE_BLOCK, d), h.dtype), pltpu.SemaphoreType.DMA(()),
                        pltpu.SemaphoreType.DMA(())],
        compiler_params=_cparams("arbitrary"),
        name="moe_dispatch",
    )(code_flat, pad_start, fill, h.reshape(n // ISSUE_UNROLL, ISSUE_UNROLL, d))


def _expert_kernel(be_ref, nb_ref, xs_ref, wg_ref, wu_ref, wd_ref, ys_ref, wg_bf, wu_bf, wd_bf):
    i = pl.program_id(0)
    used = i < nb_ref[0]

    @pl.when(used & ((i == 0) | (be_ref[i] != be_ref[jnp.maximum(i - 1, 0)])))
    def _():
        wg_bf[...] = wg_ref[0].astype(BF16)
        wu_bf[...] = wu_ref[0].astype(BF16)
        wd_bf[...] = wd_ref[0].astype(BF16)

    @pl.when(used)
    def _():
        xb = xs_ref[...].astype(BF16)
        g = _dot(xb, wg_bf[...])
        u = _dot(xb, wu_bf[...])
        hdn = (g * jax.nn.sigmoid(g)) * u
        ys_ref[...] = _dot(hdn.astype(BF16), wd_bf[...])

    @pl.when(jnp.logical_not(used))
    def _():
        ys_ref[...] = jnp.zeros_like(ys_ref)


def _expert_call(blk_expert, n_used, xs, wg, wu, wd):
    cap, d = xs.shape
    f = wg.shape[2]
    nblk = cap // MOE_BLOCK
    return pl.pallas_call(
        _expert_kernel,
        grid_spec=pltpu.PrefetchScalarGridSpec(
            num_scalar_prefetch=2,
            grid=(nblk,),
            in_specs=[
                pl.BlockSpec((MOE_BLOCK, d), lambda i, be, nb: (i, 0)),
                pl.BlockSpec((1, d, f), lambda i, be, nb: (be[i], 0, 0)),
                pl.BlockSpec((1, d, f), lambda i, be, nb: (be[i], 0, 0)),
                pl.BlockSpec((1, f, d), lambda i, be, nb: (be[i], 0, 0)),
            ],
            out_specs=pl.BlockSpec((MOE_BLOCK, d), lambda i, be, nb: (i, 0)),
            scratch_shapes=[pltpu.VMEM((d, f), BF16), pltpu.VMEM((d, f), BF16),
                            pltpu.VMEM((f, d), BF16)],
        ),
        out_shape=jax.ShapeDtypeStruct((cap, d), F32),
        compiler_params=_cparams("arbitrary"),
        name="moe_experts",
    )(blk_expert, n_used, xs, wg, wu, wd)


def _combine_kernel(code_ref, next_code_ref, start_ref, base_ref, route_ref, g2_ref, b2_ref, ys_ref,
                    o_ref, buf_ref, sems):
    tt = base_ref.shape[0]
    groups = tt // ISSUE_UNROLL
    i = pl.program_id(0)
    last = pl.num_programs(0) - 1

    def copy(codes, buf, g, u, k):
        dest = _slot_dest(codes, start_ref, TOP_K * (g * ISSUE_UNROLL + u) + k)
        return pltpu.make_async_copy(ys_ref.at[pl.ds(dest, 1), :],
                                     buf_ref.at[buf, k * groups + g, pl.ds(u, 1), :], sems.at[buf])

    def for_all_rows(codes, buf, wait):
        def body(g, c):
            for u in range(ISSUE_UNROLL):
                for k in range(TOP_K):
                    cp = copy(codes, buf, g, u, k)
                    if wait:
                        cp.wait()
                    else:
                        cp.start(priority=k)
            return c
        lax.fori_loop(0, groups, body, 0)

    cur = i % 2

    @pl.when(i == 0)
    def _():
        for_all_rows(code_ref, 0, False)

    @pl.when(i < last)
    def _():
        for_all_rows(next_code_ref, 1 - cur, False)

    for_all_rows(code_ref, cur, True)

    route = route_ref[...]
    d = base_ref.shape[1]
    y0 = buf_ref[cur, 0:groups].reshape(tt, d)
    y1 = buf_ref[cur, groups:2 * groups].reshape(tt, d)
    ffn = route[:, 2:3] * y0 + route[:, 3:4] * y1
    o_ref[...] = _layer_norm(base_ref[...] + ffn, g2_ref[...], b2_ref[...])


def _combine_call(code_flat, pad_start, base, route, g2, b2, ys):
    n, d = base.shape
    tt = min(MOVE_TILE, n)
    steps = n // tt
    return pl.pallas_call(
        _combine_kernel,
        grid=(steps,),
        in_specs=[
            pl.BlockSpec((TOP_K * tt,), lambda i: (i,), memory_space=pltpu.SMEM),
            pl.BlockSpec((TOP_K * tt,), lambda i: (jnp.minimum(i + 1, steps - 1),),
                         memory_space=pltpu.SMEM),
            pl.BlockSpec(memory_space=pltpu.SMEM),
            pl.BlockSpec((tt, d), lambda i: (i, 0)),
            pl.BlockSpec((tt, LANES), lambda i: (i, 0)),
            pl.BlockSpec((1, d), lambda i: (0, 0)),
            pl.BlockSpec((1, d), lambda i: (0, 0)),
            pl.BlockSpec(memory_space=pl.ANY),
        ],
        out_specs=pl.BlockSpec((tt, d), lambda i: (i, 0)),
        out_shape=jax.ShapeDtypeStruct((n, d), F32),
        scratch_shapes=[pltpu.VMEM((2, TOP_K * tt // ISSUE_UNROLL, ISSUE_UNROLL, d), F32),
                        pltpu.SemaphoreType.DMA((2,))],
        compiler_params=_cparams("arbitrary"),
        name="moe_combine",
    )(code_flat, code_flat, pad_start, base, route, g2, b2, ys)


def _layer(h, p_i, w_in, b_forget, w_out, g1, b1, w_group, b_group, w_router, b_router,
           w_gate, w_up, w_down, w_ple, w_ple_gate, g2, b2, alpha):
    bsz, seq, d = h.shape
    n = bsz * seq
    x2 = h.reshape(n, d)

    wr = jnp.pad(jnp.concatenate([w_group, w_router], axis=1).astype(F32),
                 ((0, 0), (0, LANES - N_GROUPS - N_EXPERTS)))
    wr_hi = wr.astype(BF16)
    wr = jnp.concatenate([wr_hi, (wr - wr_hi.astype(F32)).astype(BF16)], axis=1)
    br = jnp.pad(jnp.concatenate([b_group, b_router]).astype(F32),
                 (0, LANES - N_GROUPS - N_EXPERTS)).reshape(1, LANES)

    qsb, qfx, kaug, vt, cfmin = _proj_call(x2, w_in, b_forget, seq)
    per_tile = min(ROW_TILE, seq) // ATT_BLOCK
    cfmin = cfmin[:, :per_tile, :FOX_HEADS].reshape(bsz, seq // ATT_BLOCK, FOX_HEADS)
    cfmin = cfmin.transpose(0, 2, 1).reshape(-1)

    sb = _sb_call(qsb.reshape(bsz, seq, SB_COLS)).reshape(n, SB_WIDTH)
    fx = _fox_call(qfx.reshape(bsz, seq, FOX_WIDTH), kaug.reshape(bsz, seq, FOX_HEADS * LANES),
                   vt, cfmin).reshape(n, FOX_WIDTH)

    base, h1, route, counts = _post_call(
        sb, fx, x2, p_i.reshape(n, -1), w_out.astype(BF16), g1.reshape(1, d), b1.reshape(1, d),
        wr, br, w_ple_gate.astype(BF16), w_ple.astype(BF16), alpha)

    counts = counts[0, ROUTE_OFF:ROUTE_OFF + N_EXPERTS].astype(jnp.int32)
    padded = ((counts + MOE_BLOCK - 1) // MOE_BLOCK) * MOE_BLOCK
    pad_end = jnp.cumsum(padded)
    pad_start = pad_end - padded
    eid = route[:, 0:TOP_K].astype(jnp.int32)
    rank = route[:, 4:4 + TOP_K].astype(jnp.int32)
    code = (eid * (1 << RANK_BITS) + rank).reshape(n * TOP_K)
    cap = n * TOP_K + N_EXPERTS * MOE_BLOCK
    nblk = cap // MOE_BLOCK
    blk_first = jnp.arange(nblk, dtype=jnp.int32)[:, None] * MOE_BLOCK
    blk_expert = jnp.minimum(jnp.sum((pad_end[None, :] <= blk_first).astype(jnp.int32), axis=1),
                             N_EXPERTS - 1)
    n_used = (pad_end[-1:] // MOE_BLOCK).astype(jnp.int32)
    fill = jnp.concatenate([pad_start + counts, padded - counts, pad_end[-1:],
                            (cap - pad_end[-1:]) // MOE_BLOCK]).astype(jnp.int32)

    xs = _dispatch_call(h1, code, pad_start, fill, cap)
    ys = _expert_call(blk_expert, n_used, xs, w_gate, w_up, w_down)
    out = _combine_call(code, pad_start, base, route, g2.reshape(1, d), b2.reshape(1, d), ys)
    return out.reshape(bsz, seq, d)


def kernel(x, p, w_in, b_forget, w_out, ln_mix_g, ln_mix_b, w_group, b_group, w_router, b_router,
           w_gate, w_up, w_down, w_ple, w_ple_gate, ln_ffn_g, ln_ffn_b):
    depth = w_in.shape[0]
    alpha = (2 * depth) ** 0.25
    h = x
    for i in range(depth):
        h = _layer(h, p[i], w_in[i], b_forget[i], w_out[i], ln_mix_g[i], ln_mix_b[i], w_group[i],
                   b_group[i], w_router[i], b_router[i], w_gate[i], w_up[i], w_down[i], w_ple[i],
                   w_ple_gate[i], ln_ffn_g[i], ln_ffn_b[i], alpha)
    return h
```

```python
import functools

import jax
import jax.numpy as jnp
from jax import lax
from jax.experimental import pallas as pl
from jax.experimental.pallas import tpu as pltpu

F32 = jnp.float32
BF16 = jnp.bfloat16

HEAD_DIM = 64
SB_HEADS = 8
FOX_HEADS = 8
SB_WIDTH = SB_HEADS * HEAD_DIM
FOX_WIDTH = FOX_HEADS * HEAD_DIM
N_GROUPS = 4
EXPERTS_PER_GROUP = 8
N_EXPERTS = N_GROUPS * EXPERTS_PER_GROUP
TOP_K = 2
LN_EPS = 1e-5
SCALE = HEAD_DIM ** -0.5

LANES = 128
ATT_BLOCK = 256
ROW_TILE = 512
POST_TILE = 512
POST_SUB = 256
MOE_BLOCK = 256
MOVE_TILE = 256
DISPATCH_TILE = 512
ROUTE_OFF = N_GROUPS
VMEM_LIMIT = 56 * 1024 * 1024


def _cparams(*sem):
    return pltpu.CompilerParams(dimension_semantics=sem, vmem_limit_bytes=VMEM_LIMIT)


def _dot(a, b):
    return jnp.dot(a, b, preferred_element_type=F32)


def _dot_nt(a, b):
    return lax.dot_general(a, b, (((1,), (1,)), ((), ())), preferred_element_type=F32)


def _split2(x):
    hi = x.astype(BF16)
    lo = (x - hi.astype(F32)).astype(BF16)
    return hi, lo


def _split3(x):
    hi = x.astype(BF16)
    r = x - hi.astype(F32)
    mid = r.astype(BF16)
    lo = (r - mid.astype(F32)).astype(BF16)
    return hi, mid, lo


def _layer_norm(u, g, b):
    mu = jnp.mean(u, axis=-1, keepdims=True)
    d = u - mu
    var = jnp.mean(d * d, axis=-1, keepdims=True)
    return d * lax.rsqrt(var + LN_EPS) * g + b


SB_COLS = 3 * SB_WIDTH
FXK_OFF = SB_COLS + FOX_WIDTH
FGATE_OFF = FXK_OFF + FOX_HEADS * LANES
PROJ_COLS = FGATE_OFF + LANES
AUG_LANE = HEAD_DIM
AUG_TERMS = 3


def _proj_kernel(x_ref, w_ref, wvt_ref, bf_ref, tri_ref, sel_ref, qsb_ref, qfx_ref, kaug_ref,
                 vt_ref, cfmin_ref, carry_ref, *, tiles_per_seq):
    i = pl.program_id(0)
    t = ATT_BLOCK

    @pl.when(i % tiles_per_seq == 0)
    def _():
        carry_ref[...] = jnp.zeros_like(carry_ref)

    xb = x_ref[...].astype(BF16)
    tm = xb.shape[0]
    chunk = 512
    for c in range(SB_COLS // chunk):
        sl = slice(c * chunk, (c + 1) * chunk)
        res = _dot(xb, w_ref[:, sl])
        if SB_WIDTH <= c * chunk < 2 * SB_WIDTH:
            res = res * LOG2E
        qsb_ref[:, sl] = res.astype(BF16)
    qfx_ref[...] = _dot(xb, w_ref[:, SB_COLS:FXK_OFF]).astype(BF16)

    fl = _dot(xb, w_ref[:, FGATE_OFF:]) + bf_ref[...]
    lf = jnp.minimum(fl, 0.0) - jnp.log(1.0 + jnp.exp(-jnp.abs(fl)))
    tri = tri_ref[...]
    hi, mid, lo = _split3(lf)
    cum = _dot(tri, hi) + _dot(tri, mid) + _dot(tri, lo) + carry_ref[...]
    carry_ref[...] = cum[-1:, :]

    cum2 = cum * LOG2E
    nhi, nmid, nlo = _split3(-cum2)
    lane = lax.broadcasted_iota(jnp.int32, (tm, LANES), 1)
    terms = jnp.where(lane < FOX_HEADS, nhi, jnp.where(lane < 2 * FOX_HEADS, nmid, nlo))
    aug = _dot(terms, sel_ref[...])
    for c in range((FGATE_OFF - FXK_OFF) // chunk):
        sl = slice(c * chunk, (c + 1) * chunk)
        wsl = slice(FXK_OFF + c * chunk, FXK_OFF + (c + 1) * chunk)
        kaug_ref[:, sl] = (_dot(xb, w_ref[:, wsl]) * LOG2E + aug[:, sl]).astype(BF16)

    mins = []
    for c in range(tm // t):
        rows = slice(c * t, (c + 1) * t)
        vt_ref[c] = _dot_nt(wvt_ref[...], xb[rows, :]).astype(BF16)
        mins.append(jnp.min(cum2[rows, :], axis=0, keepdims=True))
    mins.append(jnp.zeros((8 - tm // t, LANES), F32))
    cfmin_ref[0] = jnp.concatenate(mins, axis=0)


def _proj_weights(w_in, b_forget):
    d = w_in.shape[0]
    off_q_fx = SB_COLS
    off_k_fx = off_q_fx + FOX_WIDTH
    off_v_fx = off_k_fx + FOX_WIDTH
    off_f = off_v_fx + FOX_WIDTH
    wk = w_in[:, off_k_fx:off_v_fx].reshape(d, FOX_HEADS, HEAD_DIM)
    wk = jnp.pad(wk, ((0, 0), (0, 0), (0, LANES - HEAD_DIM))).reshape(d, FOX_HEADS * LANES)
    wf = jnp.tile(w_in[:, off_f:off_f + FOX_HEADS], (1, AUG_TERMS))
    wf = jnp.pad(wf, ((0, 0), (0, LANES - AUG_TERMS * FOX_HEADS)))
    w_pad = jnp.concatenate([w_in[:, :off_k_fx], wk, wf], axis=1).astype(BF16)
    wvt = w_in[:, off_v_fx:off_f].T.astype(BF16)
    bf = jnp.pad(jnp.tile(b_forget.astype(F32), AUG_TERMS), (0, LANES - AUG_TERMS * FOX_HEADS))
    src = lax.broadcasted_iota(jnp.int32, (LANES, FOX_HEADS * LANES), 0)
    dst = lax.broadcasted_iota(jnp.int32, (LANES, FOX_HEADS * LANES), 1)
    head, term = src % FOX_HEADS, src // FOX_HEADS
    sel = ((term < AUG_TERMS) & (dst == head * LANES + AUG_LANE + term)).astype(BF16)
    return w_pad, wvt, bf.reshape(1, LANES), sel


def _proj_call(x2, w_in, b_forget, seq):
    n, d = x2.shape
    tm = min(ROW_TILE, seq)
    t = ATT_BLOCK
    w_pad, wvt, bf_pad, sel = _proj_weights(w_in, b_forget)
    row = lax.broadcasted_iota(jnp.int32, (tm, tm), 0)
    col = lax.broadcasted_iota(jnp.int32, (tm, tm), 1)
    tri = (col <= row).astype(BF16)
    const = lambda i: (0, 0)
    tile = lambda i: (i, 0)
    return pl.pallas_call(
        functools.partial(_proj_kernel, tiles_per_seq=seq // tm),
        grid=(n // tm,),
        in_specs=[
            pl.BlockSpec((tm, d), tile),
            pl.BlockSpec((d, PROJ_COLS), const),
            pl.BlockSpec((FOX_WIDTH, d), const),
            pl.BlockSpec((1, LANES), const),
            pl.BlockSpec((tm, tm), const),
            pl.BlockSpec(sel.shape, const),
        ],
        out_specs=[
            pl.BlockSpec((tm, SB_COLS), tile),
            pl.BlockSpec((tm, FOX_WIDTH), tile),
            pl.BlockSpec((tm, FOX_HEADS * LANES), tile),
            pl.BlockSpec((tm // t, FOX_WIDTH, t), lambda i: (i, 0, 0)),
            pl.BlockSpec((1, 8, LANES), lambda i: (i, 0, 0)),
        ],
        out_shape=[
            jax.ShapeDtypeStruct((n, SB_COLS), BF16),
            jax.ShapeDtypeStruct((n, FOX_WIDTH), BF16),
            jax.ShapeDtypeStruct((n, FOX_HEADS * LANES), BF16),
            jax.ShapeDtypeStruct((n // t, FOX_WIDTH, t), BF16),
            jax.ShapeDtypeStruct((n // tm, 8, LANES), F32),
        ],
        scratch_shapes=[pltpu.VMEM((1, LANES), F32)],
        compiler_params=_cparams("arbitrary"),
        name="proj",
    )(x2, w_pad, wvt, bf_pad, tri, sel)


ZERO_EXP2 = 152.0
LOG2E = 1.4426950408889634


def _softplus2(z2):
    return jnp.maximum(z2, 0.0) + jnp.log2(1.0 + jnp.exp2(-jnp.abs(z2)))


def _key_absmax(k_ref, kam_ref):
    seq, width = k_ref.shape[1], k_ref.shape[2]
    rows = min(512, seq)

    def body(i, m):
        blk = k_ref[0, pl.ds(pl.multiple_of(i * rows, rows), rows), :]
        return jnp.maximum(m, jnp.max(jnp.abs(blk.astype(F32)), axis=0, keepdims=True))

    kam = lax.fori_loop(0, seq // rows, body, jnp.zeros((1, width), F32))
    kam_ref[...] = jnp.broadcast_to(kam, kam_ref.shape)


def _key_sqnorm_max(k_ref, out_ref):
    seq, width = k_ref.shape[1], k_ref.shape[2]
    rows = min(512, seq)
    groups = width // LANES
    lane = lax.broadcasted_iota(jnp.int32, (rows, LANES), 1)

    def body(i, m):
        new = []
        for g in range(groups):
            blk = k_ref[0, pl.ds(pl.multiple_of(i * rows, rows), rows), g * LANES:(g + 1) * LANES]
            sq = jnp.where(lane < HEAD_DIM, jnp.square(blk.astype(F32)), 0.0)
            norms = jnp.sum(sq, axis=-1, keepdims=True)
            new.append(jnp.maximum(m[g], jnp.max(norms, axis=0, keepdims=True)))
        return tuple(new)

    best = lax.fori_loop(0, seq // rows, body, (jnp.zeros((1, 1), F32),) * groups)
    for g in range(groups):
        out_ref[:, g * LANES:(g + 1) * LANES] = jnp.broadcast_to(best[g], (out_ref.shape[0], LANES))


def _score_bound(qh, kam):
    zmax = jnp.sum(jnp.abs(qh.astype(F32)) * kam, axis=-1, keepdims=True)
    return zmax * (1.0 + 2.0 ** -10) + 2.0 ** -10


SB_QBLOCKS = 2


def _sb_kernel(q_ref, k_ref, v_ref, tri_ref, o_ref, kam_ref):
    t = ATT_BLOCK
    step = pl.program_id(2)

    @pl.when(step == 0)
    def _():
        _key_absmax(k_ref, kam_ref)

    lane = lax.broadcasted_iota(jnp.int32, (t, LANES), 1)
    row = lax.broadcasted_iota(jnp.int32, (t, t), 0)
    col = lax.broadcasted_iota(jnp.int32, (t, t), 1)
    earlier = col < row
    tri = tri_ref[...]
    kam = kam_ref[0:1, :]

    def block(qs, kb, state, diag, valid=None):
        start = pl.multiple_of(kb * t, t)
        kblk = k_ref[0, pl.ds(start, t), :]
        vblk = v_ref[0, pl.ds(start, t), :]
        new = []
        for h in range(2):
            carry, acc = state[2 * h], state[2 * h + 1]
            z = _dot_nt(qs[h], kblk)
            sp = _softplus2(z)
            if diag:
                sp = jnp.where(earlier, sp, 0.0)
            if valid is not None:
                sp = jnp.where(valid, sp, 0.0)
            hi, lo = _split2(sp)
            cum = _dot(hi, tri) + _dot(lo, tri)
            w = jnp.exp2(z - cum - carry)
            if diag:
                w = jnp.where(earlier, w, 0.0)
            if valid is not None:
                w = jnp.where(valid, w, 0.0)
            acc = acc + _dot(w.astype(BF16), vblk)
            carry = carry + cum[:, 0:1]
            new += [carry, acc]
        return tuple(new)

    init = (jnp.zeros((t, 1), F32), jnp.zeros((t, LANES), F32)) * 2
    streams = []
    for x in range(SB_QBLOCKS):
        qi = step * SB_QBLOCKS + x
        q = q_ref[0, x * t:(x + 1) * t, :] * jnp.asarray(SCALE, BF16)
        zero_q = jnp.zeros_like(q)
        qs = (jnp.where(lane < HEAD_DIM, q, zero_q), jnp.where(lane >= HEAD_DIM, q, zero_q))
        zmax = tuple(_score_bound(qh, kam) for qh in qs)
        state = block(qs, qi, init, True)
        state = block(qs, jnp.maximum(qi - 1, 0), state, False, valid=(qi >= 1) if x == 0 else None)
        streams.append((qi, qs, zmax, state))

    for x, (qi, qs, zmax, state) in enumerate(streams):
        def live(state, zmax=zmax):
            worst = jnp.maximum(jnp.max(zmax[0] - state[0]), jnp.max(zmax[1] - state[2]))
            return worst > -ZERO_EXP2

        def cond(c):
            return (c[0] >= 0) & c[1]

        def body(c, qs=qs, live=live):
            state = block(qs, c[0], c[2:], False)
            return (c[0] - 1, live(state)) + state

        c = lax.while_loop(cond, body, (qi - 2, live(state)) + state)
        o_ref[0, x * t:(x + 1) * t, :] = jnp.where(lane < HEAD_DIM, c[3], c[5]).astype(o_ref.dtype)


def _att_tri():
    t = ATT_BLOCK
    s = lax.broadcasted_iota(jnp.int32, (t, t), 0)
    j = lax.broadcasted_iota(jnp.int32, (t, t), 1)
    return (s >= j).astype(BF16)


def _sb_call(qkv):
    b, seq, _ = qkv.shape
    t = ATT_BLOCK
    pairs = SB_WIDTH // LANES
    q_off, k_off, v_off = 0, SB_WIDTH // LANES, 2 * SB_WIDTH // LANES
    rows = SB_QBLOCKS * t
    return pl.pallas_call(
        _sb_kernel,
        grid=(b, pairs, seq // rows),
        in_specs=[
            pl.BlockSpec((1, rows, LANES), lambda bi, hp, qi: (bi, qi, q_off + hp)),
            pl.BlockSpec((1, seq, LANES), lambda bi, hp, qi: (bi, 0, k_off + hp)),
            pl.BlockSpec((1, seq, LANES), lambda bi, hp, qi: (bi, 0, v_off + hp)),
            pl.BlockSpec((t, t), lambda bi, hp, qi: (0, 0)),
        ],
        out_specs=pl.BlockSpec((1, rows, LANES), lambda bi, hp, qi: (bi, qi, hp)),
        out_shape=jax.ShapeDtypeStruct((b, seq, SB_WIDTH), BF16),
        scratch_shapes=[pltpu.VMEM((8, LANES), F32)],
        compiler_params=_cparams("arbitrary", "arbitrary", "arbitrary"),
        name="sb_attn",
    )(qkv, qkv, qkv, _att_tri())


FOX_QBLOCKS = 2
FOX_ACC_ROWS = HEAD_DIM + 16


def _fox_kernel(cfmin_ref, q_ref, k_ref, vt_ref, o_ref, s_a, s_b, p_a, p_b, kam_ref, *, nblk):
    t = ATT_BLOCK
    bi, hp, step = pl.program_id(0), pl.program_id(1), pl.program_id(2)

    @pl.when(step == 0)
    def _():
        _key_sqnorm_max(k_ref, kam_ref)

    lane = lax.broadcasted_iota(jnp.int32, (t, LANES), 1)
    feat = lax.broadcasted_iota(jnp.int32, (LANES, t), 0)
    key = lax.broadcasted_iota(jnp.int32, (t, t), 0)
    qry = lax.broadcasted_iota(jnp.int32, (t, t), 1)
    visible = key <= qry
    ones_aug = jnp.where(lane < AUG_LANE + AUG_TERMS, 1.0, 0.0)
    neg_inf = jnp.full((t, t), -jnp.inf, F32)

    chains = [(x, h) for x in range(FOX_QBLOCKS) for h in range(2)]
    qis = [step * FOX_QBLOCKS + x for x in range(FOX_QBLOCKS)]
    qt, zmax = [], []
    for x, h in chains:
        qf = q_ref[0, x * t:(x + 1) * t, :].astype(F32) * SCALE
        qh = qf if h == 0 else pltpu.roll(qf, HEAD_DIM, 1)
        qht = jnp.where(lane < HEAD_DIM, qh, ones_aug).T
        qt.append(qht.astype(BF16))
        qn2 = jnp.sum(jnp.where(feat < HEAD_DIM, qht * qht, 0.0), axis=0, keepdims=True)
        kn2 = kam_ref[0:1, h * LANES:h * LANES + 1]
        zmax.append(jnp.sqrt(qn2 * kn2) * (1.0 + 2.0 ** -7) + 2.0 ** -7)

    def qk(c, off):
        h = chains[c][1]
        kb = jnp.maximum(qis[chains[c][0]] - off, 0)
        start = pl.multiple_of(kb * t, t)
        return _dot(k_ref[0, pl.ds(start, t), h * LANES:(h + 1) * LANES], qt[c])

    def softmax_part(s, m):
        m_new = jnp.maximum(m, jnp.max(s, axis=0, keepdims=True))
        return m_new, jnp.exp2(s - m_new).astype(BF16), jnp.exp2(m - m_new)

    def pv(c, off, alpha, acc, p_buf):
        h = chains[c][1]
        vt = vt_ref[jnp.maximum(qis[chains[c][0]] - off, 0), h * HEAD_DIM:(h + 1) * HEAD_DIM, :]
        vmod = jnp.concatenate([vt, jnp.ones((FOX_ACC_ROWS - HEAD_DIM, t), BF16)], axis=0)
        return alpha * acc + _dot(vmod, p_buf[c])

    def live(off, ms):
        alive = None
        for c, (x, h) in enumerate(chains):
            kb = qis[x] - off
            cf_min = cfmin_ref[(bi * FOX_HEADS + 2 * hp + h) * nblk + jnp.maximum(kb, 0)]
            a_c = (kb >= 0) & (jnp.max(zmax[c] - ms[c]) - cf_min > -ZERO_EXP2)
            alive = a_c if alive is None else alive | a_c
        return alive

    def scores(c, off, raw):
        return jnp.where(qis[chains[c][0]] - off >= 0, raw, neg_inf)

    n = len(chains)
    s_bufs, p_bufs = (s_a, s_b), (p_a, p_b)
    ms, alphas = [], []
    for c in range(n):
        m, p, alpha = softmax_part(jnp.where(visible, qk(c, 0), neg_inf), jnp.full((1, t), -jnp.inf, F32))
        p_a[c] = p
        s_b[c] = qk(c, 1)
        ms.append(m)
        alphas.append(alpha)
    accs = [jnp.zeros((FOX_ACC_ROWS, t), F32)] * n

    def cond(st):
        return (st[0] <= qis[-1]) & st[1]

    def body(st):
        j = st[0]
        ms, alphas, accs = list(st[2:2 + n]), list(st[2 + n:2 + 2 * n]), list(st[2 + 2 * n:])
        for half in range(2):
            off = j + half
            p_old, p_new = p_bufs[half], p_bufs[1 - half]
            s_cur, s_nxt = s_bufs[1 - half], s_bufs[half]
            for c in range(n):
                accs[c] = pv(c, off - 1, alphas[c], accs[c], p_old)
            for c in range(n):
                certain = half == 0 and chains[c][0] == FOX_QBLOCKS - 1
                s = s_cur[c] if certain else scores(c, off, s_cur[c])
                ms[c], p, alphas[c] = softmax_part(s, ms[c])
                p_new[c] = p
            for c in range(n):
                s_nxt[c] = qk(c, off + 1)
        return (j + 2, live(j + 2, ms)) + tuple(ms) + tuple(alphas) + tuple(accs)

    st = lax.while_loop(cond, body, (jnp.int32(1), live(1, ms)) + tuple(ms) + tuple(alphas) + tuple(accs))
    j = st[0]
    accs = [pv(c, j - 1, st[2 + n + c], st[2 + 2 * n + c], p_a) for c in range(n)]
    for x in range(FOX_QBLOCKS):
        acc0, acc1 = accs[2 * x], accs[2 * x + 1]
        out_t = jnp.concatenate([acc0[:HEAD_DIM] / acc0[HEAD_DIM:HEAD_DIM + 1],
                                 acc1[:HEAD_DIM] / acc1[HEAD_DIM:HEAD_DIM + 1]], axis=0)
        o_ref[0, x * t:(x + 1) * t, :] = out_t.T.astype(o_ref.dtype)


def _fox_call(qfx, kaug, vt, cfmin):
    b, seq, _ = qfx.shape
    t = ATT_BLOCK
    nblk = seq // t
    rows = FOX_QBLOCKS * t
    chains = 2 * FOX_QBLOCKS
    pairs = FOX_WIDTH // LANES
    return pl.pallas_call(
        functools.partial(_fox_kernel, nblk=nblk),
        grid=(b, pairs, seq // rows),
        in_specs=[
            pl.BlockSpec(memory_space=pltpu.SMEM),
            pl.BlockSpec((1, rows, LANES), lambda bi, hp, qi: (bi, qi, hp)),
            pl.BlockSpec((1, seq, 2 * LANES), lambda bi, hp, qi: (bi, 0, hp)),
            pl.BlockSpec((nblk, LANES, t), lambda bi, hp, qi: (bi, hp, 0)),
        ],
        out_specs=pl.BlockSpec((1, rows, LANES), lambda bi, hp, qi: (bi, qi, hp)),
        out_shape=jax.ShapeDtypeStruct((b, seq, FOX_WIDTH), BF16),
        scratch_shapes=[pltpu.VMEM((chains, t, t), F32), pltpu.VMEM((chains, t, t), F32),
                        pltpu.VMEM((chains, t, t), BF16), pltpu.VMEM((chains, t, t), BF16),
                        pltpu.VMEM((8, 2 * LANES), F32)],
        compiler_params=_cparams("arbitrary", "arbitrary", "arbitrary"),
        name="fox_attn",
    )(cfmin, qfx, kaug, vt)


def _post_kernel(sb_ref, fx_ref, x_ref, p_ref, wo_ref, g1_ref, b1_ref, wr_ref, br_ref, wpg_ref,
                 wp_ref, tri_ref, base_ref, h_ref, route_ref, cnt_ref, carry_ref, *, alpha):
    i = pl.program_id(0)

    @pl.when(i == 0)
    def _():
        carry_ref[...] = jnp.zeros_like(carry_ref)

    tm = POST_SUB
    for half in range(x_ref.shape[0] // tm):
        rows = slice(half * tm, (half + 1) * tm)
        _post_rows(sb_ref[rows, :], fx_ref[rows, :], x_ref[rows, :], p_ref[rows, :], wo_ref, g1_ref,
                   b1_ref, wr_ref, br_ref, wpg_ref, wp_ref, tri_ref, base_ref.at[rows, :],
                   h_ref.at[rows, :], route_ref.at[rows, :], cnt_ref, carry_ref, alpha)


def _post_rows(sb, fx, x, p, wo_ref, g1_ref, b1_ref, wr_ref, br_ref, wpg_ref, wp_ref, tri_ref,
               base_ref, h_ref, route_ref, cnt_ref, carry_ref, alpha):
    mix = _dot(sb, wo_ref[:SB_WIDTH, :]) + _dot(fx, wo_ref[SB_WIDTH:, :])
    h = _layer_norm(alpha * x + mix, g1_ref[...], b1_ref[...])
    h_ref[...] = h

    tm = h.shape[0]
    lane = lax.broadcasted_iota(jnp.int32, (tm, LANES), 1)
    big = jnp.int32(4 * LANES)
    h_hi, h_lo = _split2(h)
    hw = _dot(h_hi, wr_ref[...])
    logits = hw[:, :LANES] + hw[:, LANES:] + _dot(h_lo, wr_ref[:, :LANES]) + br_ref[...]

    def first_argmax(vals):
        vmax = jnp.max(vals, axis=-1, keepdims=True)
        idx = jnp.min(jnp.where(vals == vmax, lane, big), axis=-1, keepdims=True)
        return vmax, idx

    neg = -jnp.inf
    gl = jnp.where(lane < N_GROUPS, logits, neg)
    gmax, gsel = first_argmax(gl)
    g_prob = 1.0 / jnp.sum(jnp.exp(gl - gmax), axis=-1, keepdims=True)
    lo = ROUTE_OFF + gsel * EXPERTS_PER_GROUP
    el = jnp.where((lane >= lo) & (lane < lo + EXPERTS_PER_GROUP), logits, neg)
    v1, i1 = first_argmax(el)
    v2, i2 = first_argmax(jnp.where(lane == i1, neg, el))
    e2 = jnp.exp(v2 - v1)
    gate1 = g_prob / (1.0 + e2)
    gate2 = g_prob * e2 / (1.0 + e2)

    hit1 = lane == i1
    hit2 = lane == i2
    onehot = jnp.where(hit1 | hit2, 1.0, 0.0)
    before = _dot(tri_ref[...], onehot.astype(BF16)) + carry_ref[...]
    rank1 = jnp.sum(jnp.where(hit1, before, 0.0), axis=-1, keepdims=True)
    rank2 = jnp.sum(jnp.where(hit2, before, 0.0), axis=-1, keepdims=True)
    total = before[-1:, :] + onehot[-1:, :]
    carry_ref[...] = total
    cnt_ref[...] = jnp.broadcast_to(total, cnt_ref.shape)

    route = jnp.zeros((tm, LANES), F32)
    fields = ((i1 - ROUTE_OFF).astype(F32), (i2 - ROUTE_OFF).astype(F32), gate1, gate2, rank1, rank2)
    for k, val in enumerate(fields):
        route = jnp.where(lane == k, val, route)
    route_ref[...] = route

    ple = _dot(p.astype(BF16), wp_ref[...])
    gate = jax.nn.sigmoid(_dot(h_hi, wpg_ref[...]))
    base_ref[...] = alpha * h + ple * gate


def _post_call(sb, fx, x2, p2, wo, g1, b1, wr, br, wpg, wp, alpha):
    n, d = x2.shape
    tm = min(POST_TILE, n)
    half = POST_SUB
    row = lax.broadcasted_iota(jnp.int32, (half, half), 0)
    col = lax.broadcasted_iota(jnp.int32, (half, half), 1)
    tri = (col < row).astype(BF16)
    const = lambda i: (0, 0)
    tile = lambda i: (i, 0)
    return pl.pallas_call(
        functools.partial(_post_kernel, alpha=alpha),
        grid=(n // tm,),
        in_specs=[
            pl.BlockSpec((tm, SB_WIDTH), tile),
            pl.BlockSpec((tm, FOX_WIDTH), tile),
            pl.BlockSpec((tm, d), tile),
            pl.BlockSpec((tm, p2.shape[1]), tile),
            pl.BlockSpec(wo.shape, const),
            pl.BlockSpec((1, d), const),
            pl.BlockSpec((1, d), const),
            pl.BlockSpec(wr.shape, const),
            pl.BlockSpec((1, LANES), const),
            pl.BlockSpec(wpg.shape, const),
            pl.BlockSpec(wp.shape, const),
            pl.BlockSpec(tri.shape, const),
        ],
        out_specs=[
            pl.BlockSpec((tm, d), tile),
            pl.BlockSpec((tm, d), tile),
            pl.BlockSpec((tm, LANES), tile),
            pl.BlockSpec((8, LANES), const),
        ],
        out_shape=[
            jax.ShapeDtypeStruct((n, d), F32),
            jax.ShapeDtypeStruct((n, d), F32),
            jax.ShapeDtypeStruct((n, LANES), F32),
            jax.ShapeDtypeStruct((8, LANES), F32),
        ],
        scratch_shapes=[pltpu.VMEM((1, LANES), F32)],
        compiler_params=_cparams("arbitrary"),
        name="post_attn",
    )(sb, fx, x2, p2, wo, g1, b1, wr, br, wpg, wp, tri)


RANK_BITS = 20
ISSUE_UNROLL = 8


def _slot_dest(code_ref, start_ref, idx):
    code = code_ref[idx]
    return start_ref[lax.shift_right_logical(code, RANK_BITS)] + (code & ((1 << RANK_BITS) - 1))


def _dispatch_kernel(code_ref, start_ref, fill_ref, h_ref, xs_ref, zero_ref, sem, fill_sem, *, tt):
    @pl.when(pl.program_id(0) == 0)
    def _():
        zero_ref[...] = jnp.zeros_like(zero_ref)
        tail_start = fill_ref[2 * N_EXPERTS]
        tail_blocks = fill_ref[2 * N_EXPERTS + 1]

        def pad_row(e, r):
            return pltpu.make_async_copy(zero_ref.at[pl.ds(0, 1), :],
                                         xs_ref.at[pl.ds(fill_ref[e] + r, 1), :], fill_sem)

        def pad_group(first, g):
            rows = pl.ds(pl.multiple_of(first + g * ISSUE_UNROLL, ISSUE_UNROLL), ISSUE_UNROLL)
            return pltpu.make_async_copy(zero_ref.at[pl.ds(0, ISSUE_UNROLL), :], xs_ref.at[rows, :],
                                         fill_sem)

        def tail_block(b):
            rows = pl.ds(pl.multiple_of(tail_start + b * MOE_BLOCK, MOE_BLOCK), MOE_BLOCK)
            return pltpu.make_async_copy(zero_ref, xs_ref.at[rows, :], fill_sem)

        def for_all(wait):
            def per_expert(e, c):
                count = fill_ref[N_EXPERTS + e]
                singles = count & (ISSUE_UNROLL - 1)

                def per_row(r, c2):
                    cp = pad_row(e, r)
                    cp.wait() if wait else cp.start()
                    return c2

                def per_group(g, c2):
                    cp = pad_group(fill_ref[e] + singles, g)
                    cp.wait() if wait else cp.start()
                    return c2

                lax.fori_loop(0, singles, per_row, c)
                return lax.fori_loop(0, count // ISSUE_UNROLL, per_group, c)

            def per_block(b, c):
                cp = tail_block(b)
                cp.wait() if wait else cp.start()
                return c

            lax.fori_loop(0, N_EXPERTS, per_expert, 0)
            lax.fori_loop(0, tail_blocks, per_block, 0)

        for_all(False)
        for_all(True)

    def copy(g, u, k):
        dest = _slot_dest(code_ref, start_ref, TOP_K * (g * ISSUE_UNROLL + u) + k)
        return pltpu.make_async_copy(h_ref.at[g, pl.ds(u, 1), :], xs_ref.at[pl.ds(dest, 1), :], sem)

    def start(g, c):
        for u in range(ISSUE_UNROLL):
            for k in range(TOP_K):
                copy(g, u, k).start()
        return c

    def wait(g, c):
        for u in range(ISSUE_UNROLL):
            for k in range(TOP_K):
                copy(g, u, k).wait()
        return c

    lax.fori_loop(0, tt // ISSUE_UNROLL, start, 0)
    lax.fori_loop(0, tt // ISSUE_UNROLL, wait, 0)


def _dispatch_call(h, code_flat, pad_start, fill, cap):
    n, d = h.shape
    tt = min(DISPATCH_TILE, n)
    return pl.pallas_call(
        functools.partial(_dispatch_kernel, tt=tt),
        grid=(n // tt,),
        in_specs=[
            pl.BlockSpec((TOP_K * tt,), lambda i: (i,), memory_space=pltpu.SMEM),
            pl.BlockSpec(memory_space=pltpu.SMEM),
            pl.BlockSpec(memory_space=pltpu.SMEM),
            pl.BlockSpec((tt // ISSUE_UNROLL, ISSUE_UNROLL, d), lambda i: (i, 0, 0)),
        ],
        out_specs=pl.BlockSpec(memory_space=pl.ANY),
        out_shape=jax.ShapeDtypeStruct((cap, d), h.dtype),
        scratch_shapes=[pltpu.VMEM((MOE_BLOCK, d), h.dtype), pltpu.SemaphoreType.DMA(()),
                        pltpu.SemaphoreType.DMA(())],
        compiler_params=_cparams("arbitrary"),
        name="moe_dispatch",
    )(code_flat, pad_start, fill, h.reshape(n // ISSUE_UNROLL, ISSUE_UNROLL, d))


def _expert_kernel(be_ref, nb_ref, xs_ref, wg_ref, wu_ref, wd_ref, ys_ref, wg_bf, wu_bf, wd_bf):
    i = pl.program_id(0)
    used = i < nb_ref[0]

    @pl.when(used & ((i == 0) | (be_ref[i] != be_ref[jnp.maximum(i - 1, 0)])))
    def _():
        wg_bf[...] = wg_ref[0].astype(BF16)
        wu_bf[...] = wu_ref[0].astype(BF16)
        wd_bf[...] = wd_ref[0].astype(BF16)

    @pl.when(used)
    def _():
        xb = xs_ref[...].astype(BF16)
        g = _dot(xb, wg_bf[...])
        u = _dot(xb, wu_bf[...])
        hdn = (g * jax.nn.sigmoid(g)) * u
        ys_ref[...] = _dot(hdn.astype(BF16), wd_bf[...])

    @pl.when(jnp.logical_not(used))
    def _():
        ys_ref[...] = jnp.zeros_like(ys_ref)


def _expert_call(blk_expert, n_used, xs, wg, wu, wd):
    cap, d = xs.shape
    f = wg.shape[2]
    nblk = cap // MOE_BLOCK
    return pl.pallas_call(
        _expert_kernel,
        grid_spec=pltpu.PrefetchScalarGridSpec(
            num_scalar_prefetch=2,
            grid=(nblk,),
            in_specs=[
                pl.BlockSpec((MOE_BLOCK, d), lambda i, be, nb: (i, 0)),
                pl.BlockSpec((1, d, f), lambda i, be, nb: (be[i], 0, 0)),
                pl.BlockSpec((1, d, f), lambda i, be, nb: (be[i], 0, 0)),
                pl.BlockSpec((1, f, d), lambda i, be, nb: (be[i], 0, 0)),
            ],
            out_specs=pl.BlockSpec((MOE_BLOCK, d), lambda i, be, nb: (i, 0)),
            scratch_shapes=[pltpu.VMEM((d, f), BF16), pltpu.VMEM((d, f), BF16),
                            pltpu.VMEM((f, d), BF16)],
        ),
        out_shape=jax.ShapeDtypeStruct((cap, d), F32),
        compiler_params=_cparams("arbitrary"),
        name="moe_experts",
    )(blk_expert, n_used, xs, wg, wu, wd)


def _combine_kernel(code_ref, next_code_ref, start_ref, base_ref, route_ref, g2_ref, b2_ref, ys_ref,
                    o_ref, buf_ref, sems):
    tt = base_ref.shape[0]
    groups = tt // ISSUE_UNROLL
    i = pl.program_id(0)
    last = pl.num_programs(0) - 1

    def copy(codes, buf, g, u, k):
        dest = _slot_dest(codes, start_ref, TOP_K * (g * ISSUE_UNROLL + u) + k)
        return pltpu.make_async_copy(ys_ref.at[pl.ds(dest, 1), :],
                                     buf_ref.at[buf, k * groups + g, pl.ds(u, 1), :], sems.at[buf])

    def for_all_rows(codes, buf, wait):
        def body(g, c):
            for u in range(ISSUE_UNROLL):
                for k in range(TOP_K):
                    cp = copy(codes, buf, g, u, k)
                    if wait:
                        cp.wait()
                    else:
                        cp.start(priority=k)
            return c
        lax.fori_loop(0, groups, body, 0)

    cur = i % 2

    @pl.when(i == 0)
    def _():
        for_all_rows(code_ref, 0, False)

    @pl.when(i < last)
    def _():
        for_all_rows(next_code_ref, 1 - cur, False)

    for_all_rows(code_ref, cur, True)

    route = route_ref[...]
    d = base_ref.shape[1]
    y0 = buf_ref[cur, 0:groups].reshape(tt, d)
    y1 = buf_ref[cur, groups:2 * groups].reshape(tt, d)
    ffn = route[:, 2:3] * y0 + route[:, 3:4] * y1
    o_ref[...] = _layer_norm(base_ref[...] + ffn, g2_ref[...], b2_ref[...])


def _combine_call(code_flat, pad_start, base, route, g2, b2, ys):
    n, d = base.shape
    tt = min(MOVE_TILE, n)
    steps = n // tt
    return pl.pallas_call(
        _combine_kernel,
        grid=(steps,),
        in_specs=[
            pl.BlockSpec((TOP_K * tt,), lambda i: (i,), memory_space=pltpu.SMEM),
            pl.BlockSpec((TOP_K * tt,), lambda i: (jnp.minimum(i + 1, steps - 1),),
                         memory_space=pltpu.SMEM),
            pl.BlockSpec(memory_space=pltpu.SMEM),
            pl.BlockSpec((tt, d), lambda i: (i, 0)),
            pl.BlockSpec((tt, LANES), lambda i: (i, 0)),
            pl.BlockSpec((1, d), lambda i: (0, 0)),
            pl.BlockSpec((1, d), lambda i: (0, 0)),
            pl.BlockSpec(memory_space=pl.ANY),
        ],
        out_specs=pl.BlockSpec((tt, d), lambda i: (i, 0)),
        out_shape=jax.ShapeDtypeStruct((n, d), F32),
        scratch_shapes=[pltpu.VMEM((2, TOP_K * tt // ISSUE_UNROLL, ISSUE_UNROLL, d), F32),
                        pltpu.SemaphoreType.DMA((2,))],
        compiler_params=_cparams("arbitrary"),
        name="moe_combine",
    )(code_flat, code_flat, pad_start, base, route, g2, b2, ys)


def _layer(h, p_i, w_in, b_forget, w_out, g1, b1, w_group, b_group, w_router, b_router,
           w_gate, w_up, w_down, w_ple, w_ple_gate, g2, b2, alpha):
    bsz, seq, d = h.shape
    n = bsz * seq
    x2 = h.reshape(n, d)

    wr = jnp.pad(jnp.concatenate([w_group, w_router], axis=1).astype(F32),
                 ((0, 0), (0, LANES - N_GROUPS - N_EXPERTS)))
    wr_hi = wr.astype(BF16)
    wr = jnp.concatenate([wr_hi, (wr - wr_hi.astype(F32)).astype(BF16)], axis=1)
    br = jnp.pad(jnp.concatenate([b_group, b_router]).astype(F32),
                 (0, LANES - N_GROUPS - N_EXPERTS)).reshape(1, LANES)

    qsb, qfx, kaug, vt, cfmin = _proj_call(x2, w_in, b_forget, seq)
    per_tile = min(ROW_TILE, seq) // ATT_BLOCK
    cfmin = cfmin[:, :per_tile, :FOX_HEADS].reshape(bsz, seq // ATT_BLOCK, FOX_HEADS)
    cfmin = cfmin.transpose(0, 2, 1).reshape(-1)

    sb = _sb_call(qsb.reshape(bsz, seq, SB_COLS)).reshape(n, SB_WIDTH)
    fx = _fox_call(qfx.reshape(bsz, seq, FOX_WIDTH), kaug.reshape(bsz, seq, FOX_HEADS * LANES),
                   vt, cfmin).reshape(n, FOX_WIDTH)

    base, h1, route, counts = _post_call(
        sb, fx, x2, p_i.reshape(n, -1), w_out.astype(BF16), g1.reshape(1, d), b1.reshape(1, d),
        wr, br, w_ple_gate.astype(BF16), w_ple.astype(BF16), alpha)

    counts = counts[0, ROUTE_OFF:ROUTE_OFF + N_EXPERTS].astype(jnp.int32)
    padded = ((counts + MOE_BLOCK - 1) // MOE_BLOCK) * MOE_BLOCK
    pad_end = jnp.cumsum(padded)
    pad_start = pad_end - padded
    eid = route[:, 0:TOP_K].astype(jnp.int32)
    rank = route[:, 4:4 + TOP_K].astype(jnp.int32)
    code = (eid * (1 << RANK_BITS) + rank).reshape(n * TOP_K)
    cap = n * TOP_K + N_EXPERTS * MOE_BLOCK
    nblk = cap // MOE_BLOCK
    blk_first = jnp.arange(nblk, dtype=jnp.int32)[:, None] * MOE_BLOCK
    blk_expert = jnp.minimum(jnp.sum((pad_end[None, :] <= blk_first).astype(jnp.int32), axis=1),
                             N_EXPERTS - 1)
    n_used = (pad_end[-1:] // MOE_BLOCK).astype(jnp.int32)
    fill = jnp.concatenate([pad_start + counts, padded - counts, pad_end[-1:],
                            (cap - pad_end[-1:]) // MOE_BLOCK]).astype(jnp.int32)

    xs = _dispatch_call(h1, code, pad_start, fill, cap)
    ys = _expert_call(blk_expert, n_used, xs, w_gate, w_up, w_down)
    out = _combine_call(code, pad_start, base, route, g2.reshape(1, d), b2.reshape(1, d), ys)
    return out.reshape(bsz, seq, d)


def kernel(x, p, w_in, b_forget, w_out, ln_mix_g, ln_mix_b, w_group, b_group, w_router, b_router,
           w_gate, w_up, w_down, w_ple, w_ple_gate, ln_ffn_g, ln_ffn_b):
    depth = w_in.shape[0]
    alpha = (2 * depth) ** 0.25
    h = x
    for i in range(depth):
        h = _layer(h, p[i], w_in[i], b_forget[i], w_out[i], ln_mix_g[i], ln_mix_b[i], w_group[i],
                   b_group[i], w_router[i], b_router[i], w_gate[i], w_up[i], w_down[i], w_ple[i],
                   w_ple_gate[i], ln_ffn_g[i], ln_ffn_b[i], alpha)
    return h
```
